```python
import math
import jax, jax.numpy as jnp
from jax import lax
import numpy as np

D_MODEL = 1024
BATCH = 8
SEQ = 4096
DEPTH = 4

N_MIXERS = 2
D_FF = 2816
NORM_EPS = 1e-6
CHUNK = 64

HG_DK = 128
HG_HEADS = D_MODEL // HG_DK
HG_DV = D_MODEL // HG_HEADS
HG_IN = 4 * D_MODEL
N_HG_LAYERS = (DEPTH + 1) // 2

GD_QK_HEADS = 8
GD_V_HEADS = 16
GD_DK = 128
GD_DV = 128
GD_KEY = GD_QK_HEADS * GD_DK
GD_VAL = GD_V_HEADS * GD_DV
GD_QKV = 2 * GD_KEY + GD_VAL
GD_CONV = 4
GD_IN = GD_QKV + GD_VAL + 2 * GD_V_HEADS
N_GD_LAYERS = DEPTH // 2

kernel_name = "hgrn2_gdn_macaron_sandwich_trunk"


def _rmsnorm(x, gain):
    xf = x.astype(jnp.float32)
    y = xf * lax.rsqrt(jnp.mean(xf * xf, axis=-1, keepdims=True) + NORM_EPS)
    return (y * gain.astype(jnp.float32)).astype(x.dtype)


def _gated_rmsnorm(o, gate, gain):
    y = o * lax.rsqrt(jnp.mean(o * o, axis=-1, keepdims=True) + NORM_EPS)
    return y * gain.astype(jnp.float32) * jax.nn.silu(gate)


def _l2norm(x):
    return x * lax.rsqrt(jnp.sum(x * x, axis=-1, keepdims=True) + NORM_EPS)


def _swiglu(x, w_in, w_out):
    gate, up = jnp.split(x @ w_in, 2, axis=-1)
    return (jax.nn.silu(gate) * up) @ w_out


def _to_chunks(x):
    B, T, H, d = x.shape
    return x.reshape(B, T // CHUNK, CHUNK, H, d).transpose(0, 3, 1, 2, 4)


def _to_chunks_scalar(x):
    B, T, H = x.shape
    return x.reshape(B, T // CHUNK, CHUNK, H).transpose(0, 3, 1, 2)


def _from_chunks(x):
    B, H, N, C, d = x.shape
    return x.transpose(0, 2, 3, 1, 4).reshape(B, N * C, H, d)


def _causal_conv(x, w):
    K, C = w.shape
    return lax.conv_general_dilated(
        x, w[:, None, :], window_strides=(1,), padding=[(K - 1, 0)],
        dimension_numbers=("NWC", "WIO", "NWC"), feature_group_count=C)


def _hgrn2(h, w_in, lower_bound, norm_gain, w_out):
    B, T, _ = h.shape
    q, f, i, g = jnp.split((h @ w_in).astype(jnp.float32), 4, axis=-1)
    q = jax.nn.silu(q)
    forget = lower_bound + (1.0 - lower_bound) * jax.nn.sigmoid(f)
    k = 1.0 - forget
    log_f = jnp.log(forget)
    q_c, k_c, v_c, lf_c = [_to_chunks(t.reshape(B, T, HG_HEADS, -1)) for t in (q, k, i, log_f)]
    b = jnp.cumsum(lf_c, axis=-2)
    b_mid = b[..., CHUNK // 2:CHUNK // 2 + 1, :]
    causal = jnp.tril(jnp.ones((CHUNK, CHUNK), dtype=bool))
    scores = jnp.einsum("bhnck,bhnsk->bhncs", q_c * jnp.exp(b - b_mid), k_c * jnp.exp(b_mid - b))
    o_intra = jnp.einsum("bhncs,bhnsv->bhncv", jnp.where(causal, scores, 0.0), v_c)
    b_last = b[..., -1:, :]
    q_inter = q_c * jnp.exp(b)
    k_state = k_c * jnp.exp(b_last - b)
    decay = jnp.exp(b_last[..., 0, :])

    def step(S, xs):
        qn, kn, vn, dn = xs
        o = jnp.einsum("bhck,bhkv->bhcv", qn, S)
        S = dn[..., :, None] * S + jnp.einsum("bhsk,bhsv->bhkv", kn, vn)
        return S, o

    S0 = jnp.zeros((B, HG_HEADS, HG_DK, HG_DV), jnp.float32)
    xs = tuple(jnp.moveaxis(t, 2, 0) for t in (q_inter, k_state, v_c, decay))
    _, o_inter = lax.scan(step, S0, xs)
    o = _from_chunks(o_intra + jnp.moveaxis(o_inter, 0, 2))
    o = _gated_rmsnorm(o, g.reshape(B, T, HG_HEADS, HG_DV), norm_gain)
    return o.reshape(B, T, HG_HEADS * HG_DV).astype(h.dtype) @ w_out


def _gated_deltanet(h, w_in, conv_w, a_log, dt_bias, norm_gain, w_out):
    B, T, _ = h.shape
    proj = (h @ w_in).astype(jnp.float32)
    qkv = jax.nn.silu(_causal_conv(proj[..., :GD_QKV], conv_w.astype(jnp.float32)))
    z = proj[..., GD_QKV:GD_QKV + GD_VAL]
    beta = jax.nn.sigmoid(proj[..., GD_QKV + GD_VAL:GD_QKV + GD_VAL + GD_V_HEADS])
    a = proj[..., GD_QKV + GD_VAL + GD_V_HEADS:]
    g = -jnp.exp(a_log.astype(jnp.float32)) * jax.nn.softplus(a + dt_bias)
    rep = GD_V_HEADS // GD_QK_HEADS
    q = _l2norm(qkv[..., :GD_KEY].reshape(B, T, GD_QK_HEADS, GD_DK)) * (GD_DK ** -0.5)
    k = _l2norm(qkv[..., GD_KEY:2 * GD_KEY].reshape(B, T, GD_QK_HEADS, GD_DK))
    v = qkv[..., 2 * GD_KEY:].reshape(B, T, GD_V_HEADS, GD_DV)
    q_c = _to_chunks(jnp.repeat(q, rep, axis=2))
    k_c = _to_chunks(jnp.repeat(k, rep, axis=2))
    v_c = _to_chunks(v)
    beta_c = _to_chunks_scalar(beta)
    G = jnp.cumsum(_to_chunks_scalar(g), axis=-1)
    incl = jnp.tril(jnp.ones((CHUNK, CHUNK), dtype=bool))
    strict = jnp.tril(jnp.ones((CHUNK, CHUNK), dtype=bool), k=-1)
    gamma = jnp.exp(jnp.where(incl, G[..., :, None] - G[..., None, :], -jnp.inf))
    kk = jnp.einsum("bhnck,bhnsk->bhncs", k_c, k_c)
    a_mat = jnp.where(strict, kk * gamma * beta_c[..., :, None], 0.0)
    eye = jnp.eye(CHUNK, dtype=jnp.float32)
    t_mat = lax.linalg.triangular_solve(a_mat + eye, jnp.broadcast_to(eye, a_mat.shape),
                                        left_side=True, lower=True, unit_diagonal=True)
    u = jnp.einsum("bhncs,bhnsv->bhncv", t_mat, v_c * beta_c[..., None])
    w = jnp.einsum("bhncs,bhnsk->bhnck", t_mat, k_c * (beta_c * jnp.exp(G))[..., None])
    qk = jnp.einsum("bhnck,bhnsk->bhncs", q_c, k_c) * gamma
    q_g = q_c * jnp.exp(G)[..., None]
    k_g = k_c * jnp.exp(G[..., -1:] - G)[..., None]
    decay = jnp.exp(G[..., -1])

    def step(S, xs):
        un, wn, qkn, qn, kn, dn = xs
        v_new = un - jnp.einsum("bhck,bhkv->bhcv", wn, S)
        o = jnp.einsum("bhck,bhkv->bhcv", qn, S) + jnp.einsum("bhcs,bhsv->bhcv", qkn, v_new)
        S = dn[..., None, None] * S + jnp.einsum("bhsk,bhsv->bhkv", kn, v_new)
        return S, o

    S0 = jnp.zeros((B, GD_V_HEADS, GD_DK, GD_DV), jnp.float32)
    xs = tuple(jnp.moveaxis(t, 2, 0) for t in (u, w, qk, q_g, k_g, decay))
    _, o = lax.scan(step, S0, xs)
    o = _from_chunks(jnp.moveaxis(o, 0, 2))
    o = _gated_rmsnorm(o, z.reshape(B, T, GD_V_HEADS, GD_DV), norm_gain)
    return o.reshape(B, T, GD_VAL).astype(h.dtype) @ w_out


def setup_inputs(seed: int = 0) -> dict:
    key = jax.random.key(seed)
    ks = jax.random.split(key, 16)

    def nrm(k, shape, scale):
        return jax.random.normal(k, shape, jnp.float32) * scale

    dt = jnp.exp(jax.random.uniform(ks[13], (N_GD_LAYERS, GD_V_HEADS), jnp.float32,
                                    minval=math.log(1e-3), maxval=math.log(1e-1)))
    return {
        "x": nrm(ks[0], (BATCH, SEQ, D_MODEL), 1.0),
        "norm_gains": 1.0 + nrm(ks[1], (DEPTH, 6, D_MODEL), 0.02),
        "ffn1_w_in": nrm(ks[2], (DEPTH, D_MODEL, 2 * D_FF), D_MODEL ** -0.5),
        "ffn1_w_out": nrm(ks[3], (DEPTH, D_FF, D_MODEL), D_FF ** -0.5),
        "ffn2_w_in": nrm(ks[4], (DEPTH, D_MODEL, 2 * D_FF), D_MODEL ** -0.5),
        "ffn2_w_out": nrm(ks[5], (DEPTH, D_FF, D_MODEL), D_FF ** -0.5),
        "hg_w_in": nrm(ks[6], (N_HG_LAYERS, D_MODEL, HG_IN), D_MODEL ** -0.5),
        "hg_lower_bounds": 1.0 + nrm(ks[7], (N_HG_LAYERS, HG_HEADS * HG_DK), 0.1),
        "hg_norm_gain": 1.0 + nrm(ks[8], (N_HG_LAYERS, HG_DV), 0.02),
        "hg_w_out": nrm(ks[9], (N_HG_LAYERS, HG_HEADS * HG_DV, D_MODEL), D_MODEL ** -0.5),
        "gd_w_in": nrm(ks[10], (N_GD_LAYERS, D_MODEL, GD_IN), D_MODEL ** -0.5),
        "gd_conv_w": nrm(ks[11], (N_GD_LAYERS, GD_CONV, GD_QKV), GD_CONV ** -0.5),
        "gd_a_log": jnp.log(jax.random.uniform(ks[12], (N_GD_LAYERS, GD_V_HEADS), jnp.float32,
                                                minval=1.0, maxval=16.0)),
        "gd_dt_bias": dt + jnp.log(-jnp.expm1(-dt)),
        "gd_norm_gain": 1.0 + nrm(ks[14], (N_GD_LAYERS, GD_DV), 0.02),
        "gd_w_out": nrm(ks[15], (N_GD_LAYERS, GD_VAL, D_MODEL), GD_VAL ** -0.5),
    }


def reference(x, norm_gains, ffn1_w_in, ffn1_w_out, ffn2_w_in, ffn2_w_out,
              hg_w_in, hg_lower_bounds, hg_norm_gain, hg_w_out,
              gd_w_in, gd_conv_w, gd_a_log, gd_dt_bias, gd_norm_gain, gd_w_out):
    lb = jnp.cumsum(jax.nn.softmax(hg_lower_bounds.astype(jnp.float32), axis=0), axis=0)
    lb = lb - lb[0]
    for i in range(DEPTH):
        ng = norm_gains[i]
        x = x + 0.5 * _rmsnorm(_swiglu(_rmsnorm(x, ng[0]), ffn1_w_in[i], ffn1_w_out[i]), ng[1])
        h = _rmsnorm(x, ng[2])
        j = i // N_MIXERS
        if i % N_MIXERS == 0:
            m = _hgrn2(h, hg_w_in[j], lb[j], hg_norm_gain[j], hg_w_out[j])
        else:
            m = _gated_deltanet(h, gd_w_in[j], gd_conv_w[j], gd_a_log[j], gd_dt_bias[j],
                                gd_norm_gain[j], gd_w_out[j])
        x = x + _rmsnorm(m, ng[3])
        x = x + 0.5 * _rmsnorm(_swiglu(_rmsnorm(x, ng[4]), ffn2_w_in[i], ffn2_w_out[i]), ng[5])
    return x
```

```python
import functools

import jax
import jax.numpy as jnp
from jax import lax
from jax.experimental import pallas as pl
from jax.experimental.pallas import tpu as pltpu

NORM_EPS = 1e-6
CHUNK = 64
HEAD_DIM = 128
CONV_K = 4
CONV_HIST = 8
INV_BLOCK = 16
V7X_VMEM_LIMIT_BYTES = 58 * 1024 * 1024

F32 = jnp.float32
BF16 = jnp.bfloat16
HI = lax.Precision.HIGHEST


def _dot(a, b, precision=None):
    return jnp.dot(a, b, preferred_element_type=F32, precision=precision)


def _dot_nt(a, b, precision=None):
    return lax.dot_general(a, b, (((1,), (1,)), ((), ())),
                           preferred_element_type=F32, precision=precision)


def _dot_tn(a, b, precision=None):
    return lax.dot_general(a, b, (((0,), (0,)), ((), ())),
                           preferred_element_type=F32, precision=precision)


def _rms(x, gain):
    return x * lax.rsqrt(jnp.mean(x * x, axis=-1, keepdims=True) + NORM_EPS) * gain


def _tril_mask(n, k=0):
    r = lax.broadcasted_iota(jnp.int32, (n, n), 0)
    c = lax.broadcasted_iota(jnp.int32, (n, n), 1)
    return c <= r + k


def _ffn_body(x_ref, gpre_ref, gpost_ref, wg_ref, wu_ref, wo_ref, o_ref, h_ref, acc_ref):
    j = pl.program_id(1)

    @pl.when(j == 0)
    def _():
        h_ref[...] = _rms(x_ref[...], gpre_ref[...]).astype(BF16)

    h = h_ref[...]
    gate = _dot(h, wg_ref[...])
    up = _dot(h, wu_ref[...])
    act = (jax.nn.silu(gate) * up).astype(BF16)
    part = _dot(act, wo_ref[...])

    @pl.when(j == 0)
    def _():
        acc_ref[...] = part

    @pl.when(j > 0)
    def _():
        acc_ref[...] += part

    @pl.when(j == pl.num_programs(1) - 1)
    def _():
        o_ref[...] = x_ref[...] + 0.5 * _rms(acc_ref[...], gpost_ref[...])


def _ffn_call(x2, gpre, gpost, w_in, w_out, *, tm, tf):
    n, d = x2.shape
    dff = w_out.shape[0]
    nf = dff // tf
    return pl.pallas_call(
        _ffn_body,
        grid=(n // tm, nf),
        in_specs=[
            pl.BlockSpec((tm, d), lambda i, j: (i, 0)),
            pl.BlockSpec((1, d), lambda i, j: (0, 0)),
            pl.BlockSpec((1, d), lambda i, j: (0, 0)),
            pl.BlockSpec((d, tf), lambda i, j: (0, j)),
            pl.BlockSpec((d, tf), lambda i, j: (0, j + nf)),
            pl.BlockSpec((tf, d), lambda i, j: (j, 0)),
        ],
        out_specs=pl.BlockSpec((tm, d), lambda i, j: (i, 0)),
        out_shape=jax.ShapeDtypeStruct((n, d), F32),
        scratch_shapes=[pltpu.VMEM((tm, d), BF16), pltpu.VMEM((tm, d), F32)],
        compiler_params=pltpu.CompilerParams(
            dimension_semantics=("parallel", "arbitrary"),
            vmem_limit_bytes=V7X_VMEM_LIMIT_BYTES),
        name="ffn_sublayer",
    )(x2, gpre, gpost, w_in, w_in, w_out)


def _hgrn2_body(layer, x_ref, gpre_ref, gpost_ref, win_ref, lbraw_ref, ngain_ref, wout_ref,
                o_ref, proj_ref, y_ref, st_ref):
    tt, d = x_ref.shape[1], x_ref.shape[2]
    heads = d // HEAD_DIM

    @pl.when(pl.program_id(1) == 0)
    def _():
        st_ref[...] = jnp.zeros_like(st_ref)

    h = _rms(x_ref[0], gpre_ref[...]).astype(BF16)
    proj_ref[...] = _dot(h, win_ref[...])

    raw = lbraw_ref[...]
    e = jnp.exp(raw - jnp.max(raw, axis=0, keepdims=True))
    p = e / jnp.sum(e, axis=0, keepdims=True)
    first = p[0:1, :]
    cum = first
    for l in range(1, layer + 1):
        cum = cum + p[l:l + 1, :]
    lb = cum - first

    tril = _tril_mask(CHUNK).astype(F32)
    causal = _tril_mask(CHUNK)
    ngain = ngain_ref[...]

    def chunk_body(c, carry):
        rows = pl.ds(pl.multiple_of(c * CHUNK, CHUNK), CHUNK)
        q = jax.nn.silu(proj_ref[rows, pl.ds(0, d)])
        forget = lb + (1.0 - lb) * jax.nn.sigmoid(proj_ref[rows, pl.ds(d, d)])
        k = 1.0 - forget
        b = _dot(tril, jnp.log(forget), precision=HI)
        b_mid = b[CHUNK // 2:CHUNK // 2 + 1, :]
        b_last = b[CHUNK - 1:CHUNK, :]
        q_intra = q * jnp.exp(b - b_mid)
        k_intra = k * jnp.exp(b_mid - b)
        q_inter = q * jnp.exp(b)
        k_state = k * jnp.exp(b_last - b)
        decay = jnp.exp(b_last)
        for hh in range(heads):
            cs = slice(hh * HEAD_DIM, (hh + 1) * HEAD_DIM)
            lanes = pl.ds(hh * HEAD_DIM, HEAD_DIM)
            v = proj_ref[rows, pl.ds(2 * d + hh * HEAD_DIM, HEAD_DIM)]
            scores = _dot_nt(q_intra[:, cs], k_intra[:, cs])
            st = st_ref[hh]
            o = _dot(jnp.where(causal, scores, 0.0), v) + _dot_nt(q_inter[:, cs], st)
            st_ref[hh] = st * decay[:, cs] + _dot_tn(v, k_state[:, cs])
            gate = proj_ref[rows, pl.ds(3 * d + hh * HEAD_DIM, HEAD_DIM)]
            y = o * lax.rsqrt(jnp.mean(o * o, axis=-1, keepdims=True) + NORM_EPS)
            y_ref[rows, lanes] = (y * ngain * jax.nn.silu(gate)).astype(BF16)
        return carry

    lax.fori_loop(0, tt // CHUNK, chunk_body, 0)
    m = _dot(y_ref[...], wout_ref[...])
    o_ref[0] = x_ref[0] + _rms(m, gpost_ref[...])


def _hgrn2_call(x, gpre, gpost, w_in, lb_raw, ngain, w_out, *, layer, tt):
    bsz, t, d = x.shape
    heads = d // HEAD_DIM
    nl = lb_raw.shape[0]
    const = lambda b, i: (0, 0)
    return pl.pallas_call(
        functools.partial(_hgrn2_body, layer),
        grid=(bsz, t // tt),
        in_specs=[
            pl.BlockSpec((1, tt, d), lambda b, i: (b, i, 0)),
            pl.BlockSpec((1, d), const),
            pl.BlockSpec((1, d), const),
            pl.BlockSpec((d, 4 * d), const),
            pl.BlockSpec((nl, d), const),
            pl.BlockSpec((1, HEAD_DIM), const),
            pl.BlockSpec((d, d), const),
        ],
        out_specs=pl.BlockSpec((1, tt, d), lambda b, i: (b, i, 0)),
        out_shape=jax.ShapeDtypeStruct((bsz, t, d), F32),
        scratch_shapes=[
            pltpu.VMEM((tt, 4 * d), F32),
            pltpu.VMEM((tt, d), BF16),
            pltpu.VMEM((heads, HEAD_DIM, HEAD_DIM), F32),
        ],
        compiler_params=pltpu.CompilerParams(
            dimension_semantics=("parallel", "arbitrary"),
            vmem_limit_bytes=V7X_VMEM_LIMIT_BYTES),
        name="hgrn2_sublayer",
    )(x, gpre, gpost, w_in, lb_raw, ngain, w_out)


def _unit_lower_inverse_minus_eye(a, diag_blocks):
    dg = jnp.where(diag_blocks, a, 0.0)
    off = a - dg
    p = -dg
    q = p
    steps = INV_BLOCK.bit_length() - 2
    for _ in range(steps):
        p = _dot(p, p)
        q = q + p + _dot(p, q)
    x = -(off + _dot(q, off))
    x2 = _dot(x, x)
    y = x + x2 + _dot(x, x2)
    return q + y + _dot(y, q)


def _gdn_body(x_ref, gpre_ref, gpost_ref, wmain_ref, wba_ref, convw_ref, alog_ref, dtb_ref,
              ngain_ref, wout_ref, o_ref, ext_ref, qkv_ref, z_ref, y_ref, s_ref,
              gcol_ref, grow_ref, beta_ref):
    tt, d = x_ref.shape[1], x_ref.shape[2]
    vheads = s_ref.shape[0]
    key = (ext_ref.shape[1] - vheads * HEAD_DIM) // 2
    qk_heads = key // HEAD_DIM
    rep = vheads // qk_heads
    nconv = ext_ref.shape[1]
    nchunk = tt // CHUNK
    t_idx = pl.program_id(1)

    @pl.when(t_idx == 0)
    def _():
        s_ref[...] = jnp.zeros_like(s_ref)
        ext_ref[pl.ds(0, CONV_HIST), :] = jnp.zeros((CONV_HIST, nconv), F32)

    @pl.when(t_idx > 0)
    def _():
        ext_ref[pl.ds(0, CONV_HIST), :] = ext_ref[pl.ds(tt, CONV_HIST), :]

    h = _rms(x_ref[0], gpre_ref[...]).astype(BF16)
    pm = _dot(h, wmain_ref[...])
    ext_ref[pl.ds(CONV_HIST, tt), :] = pm[:, :nconv]
    z_ref[...] = pm[:, nconv:]
    pba = _dot(h, wba_ref[...])
    beta = jax.nn.sigmoid(pba[:, :vheads])
    g = -jnp.exp(alog_ref[...]) * jax.nn.softplus(pba[:, vheads:] + dtb_ref[...])
    beta_ref[...] = beta

    acc = convw_ref[0:1, :] * ext_ref[pl.ds(CONV_HIST - CONV_K + 1, tt), :]
    for j in range(1, CONV_K):
        acc = acc + convw_ref[j:j + 1, :] * ext_ref[pl.ds(CONV_HIST - CONV_K + 1 + j, tt), :]
    qkv_ref[...] = jax.nn.silu(acc)
    for hq in range(2 * qk_heads):
        lanes = pl.ds(hq * HEAD_DIM, HEAD_DIM)
        a = qkv_ref[:, lanes]
        a = a * lax.rsqrt(jnp.sum(a * a, axis=-1, keepdims=True) + NORM_EPS)
        if hq < qk_heads:
            a = a * (HEAD_DIM ** -0.5)
        qkv_ref[:, lanes] = a

    incl = _tril_mask(CHUNK)
    strict = _tril_mask(CHUNK, -1)
    tril = incl.astype(F32)
    r16 = lax.broadcasted_iota(jnp.int32, (vheads, vheads), 0)
    c16 = lax.broadcasted_iota(jnp.int32, (vheads, vheads), 1)
    eye_h = (r16 == c16).astype(F32)
    for c in range(nchunk):
        gc = g[c * CHUNK:(c + 1) * CHUNK, :]
        gcol_ref[pl.ds(c * CHUNK, CHUNK), :] = _dot(tril, gc, precision=HI)
        g_t = _dot_nt(eye_h, gc, precision=HI)
        grow_ref[c] = _dot_nt(g_t, tril, precision=HI)

    rb = lax.broadcasted_iota(jnp.int32, (CHUNK, CHUNK), 0) // INV_BLOCK
    cb = lax.broadcasted_iota(jnp.int32, (CHUNK, CHUNK), 1) // INV_BLOCK
    diag_blocks = rb == cb
    ngain = ngain_ref[...]

    def chunk_body(c, carry):
        rows = pl.ds(pl.multiple_of(c * CHUNK, CHUNK), CHUNK)
        g_cols = gcol_ref[rows, :]
        g_rows = grow_ref[c]
        betas = beta_ref[rows, :]
        for hq in range(qk_heads):
            q = qkv_ref[rows, pl.ds(hq * HEAD_DIM, HEAD_DIM)]
            k = qkv_ref[rows, pl.ds(key + hq * HEAD_DIM, HEAD_DIM)]
            kk = _dot_nt(k, k)
            qk = _dot_nt(q, k)
            k_t = k.T
            for r in range(rep):
                hv = hq * rep + r
                v = qkv_ref[rows, pl.ds(2 * key + hv * HEAD_DIM, HEAD_DIM)]
                g_col = g_cols[:, hv:hv + 1]
                g_row = g_rows[hv:hv + 1, :]
                b_col = betas[:, hv:hv + 1]
                g_last = g_row[:, CHUNK - 1:CHUNK]
                gamma = jnp.exp(jnp.where(incl, g_col - g_row, -jnp.inf))
                a_mat = jnp.where(strict, kk * gamma * b_col, 0.0)
                t_m = _unit_lower_inverse_minus_eye(a_mat, diag_blocks)
                rhs = jnp.concatenate([v * b_col, k * (b_col * jnp.exp(g_col))], axis=1)
                uw = rhs + _dot(t_m, rhs)
                u = uw[:, :HEAD_DIM]
                w = uw[:, HEAD_DIM:]
                s = s_ref[hv]
                v_new = u - _dot(w, s)
                o = _dot(q * jnp.exp(g_col), s) + _dot(qk * gamma, v_new)
                kg_t = k_t * jnp.exp(g_last - g_row)
                s_ref[hv] = jnp.exp(g_last) * s + _dot(kg_t, v_new)
                zg = z_ref[rows, pl.ds(hv * HEAD_DIM, HEAD_DIM)]
                y = o * lax.rsqrt(jnp.mean(o * o, axis=-1, keepdims=True) + NORM_EPS)
                y_ref[rows, pl.ds(hv * HEAD_DIM, HEAD_DIM)] = (y * ngain * jax.nn.silu(zg)).astype(BF16)
        return carry

    lax.fori_loop(0, nchunk, chunk_body, 0)
    m = _dot(y_ref[...], wout_ref[...])
    o_ref[0] = x_ref[0] + _rms(m, gpost_ref[...])


def _gdn_call(x, gpre, gpost, w_main, w_ba, conv_w, a_log, dt_bias, ngain, w_out, *, tt):
    bsz, t, d = x.shape
    vheads = a_log.shape[1]
    nconv = conv_w.shape[1]
    val = vheads * HEAD_DIM
    const = lambda b, i: (0, 0)
    return pl.pallas_call(
        _gdn_body,
        grid=(bsz, t // tt),
        in_specs=[
            pl.BlockSpec((1, tt, d), lambda b, i: (b, i, 0)),
            pl.BlockSpec((1, d), const),
            pl.BlockSpec((1, d), const),
            pl.BlockSpec((d, nconv + val), const),
            pl.BlockSpec((d, 2 * vheads), const),
            pl.BlockSpec((CONV_K, nconv), const),
            pl.BlockSpec((1, vheads), const),
            pl.BlockSpec((1, vheads), const),
            pl.BlockSpec((1, HEAD_DIM), const),
            pl.BlockSpec((val, d), const),
        ],
        out_specs=pl.BlockSpec((1, tt, d), lambda b, i: (b, i, 0)),
        out_shape=jax.ShapeDtypeStruct((bsz, t, d), F32),
        scratch_shapes=[
            pltpu.VMEM((tt + CONV_HIST, nconv), F32),
            pltpu.VMEM((tt, nconv), F32),
            pltpu.VMEM((tt, val), F32),
            pltpu.VMEM((tt, val), BF16),
            pltpu.VMEM((vheads, HEAD_DIM, HEAD_DIM), F32),
            pltpu.VMEM((tt, vheads), F32),
            pltpu.VMEM((tt // CHUNK, vheads, CHUNK), F32),
            pltpu.VMEM((tt, vheads), F32),
        ],
        compiler_params=pltpu.CompilerParams(
            dimension_semantics=("parallel", "arbitrary"),
            vmem_limit_bytes=V7X_VMEM_LIMIT_BYTES),
        name="gdn_sublayer",
    )(x, gpre, gpost, w_main, w_ba, conv_w, a_log, dt_bias, ngain, w_out)


def _tile(n, pref):
    t = min(n, pref)
    assert n % t == 0, (n, t)
    return t


def kernel(x, norm_gains, ffn1_w_in, ffn1_w_out, ffn2_w_in, ffn2_w_out, hg_w_in, hg_lower_bounds,
           hg_norm_gain, hg_w_out, gd_w_in, gd_conv_w, gd_a_log, gd_dt_bias, gd_norm_gain, gd_w_out):
    bsz, t, d = x.shape
    depth = norm_gains.shape[0]
    n_mixers = 2
    dff = ffn1_w_out.shape[1]
    nconv = gd_conv_w.shape[2]
    vheads = gd_a_log.shape[1]
    n_main = nconv + vheads * HEAD_DIM

    tm = _tile(bsz * t, 1024)
    tf = 256 if dff % 256 == 0 else 128
    tt_hg = _tile(t, 512)
    tt_gd = _tile(t, 256)

    def row(v):
        return v.reshape(1, -1)

    def ffn(xc, gpre, gpost, w_in, w_out):
        y = _ffn_call(xc.reshape(bsz * t, d), row(gpre), row(gpost),
                      w_in.astype(BF16), w_out.astype(BF16), tm=tm, tf=tf)
        return y.reshape(bsz, t, d)

    for i in range(depth):
        ng = norm_gains[i]
        j = i // n_mixers
        x = ffn(x, ng[0], ng[1], ffn1_w_in[i], ffn1_w_out[i])
        if i % n_mixers == 0:
            x = _hgrn2_call(x, row(ng[2]), row(ng[3]), hg_w_in[j].astype(BF16), hg_lower_bounds,
                            row(hg_norm_gain[j]), hg_w_out[j].astype(BF16), layer=j, tt=tt_hg)
        else:
            w_in = gd_w_in[j].astype(BF16)
            x = _gdn_call(x, row(ng[2]), row(ng[3]), w_in[:, :n_main], w_in[:, n_main:],
                          gd_conv_w[j], row(gd_a_log[j]), row(gd_dt_bias[j]),
                          row(gd_norm_gain[j]), gd_w_out[j].astype(BF16), tt=tt_gd)
        x = ffn(x, ng[4], ng[5], ffn2_w_in[i], ffn2_w_out[i])
    return x
```

```python
import functools

import jax
import jax.numpy as jnp
from jax import lax
from jax.experimental import pallas as pl
from jax.experimental.pallas import tpu as pltpu

NORM_EPS = 1e-6
CHUNK = 64
HEAD_DIM = 128
CONV_K = 4
CONV_HIST = 8
INV_BLOCK = 16
V7X_VMEM_LIMIT_BYTES = 58 * 1024 * 1024

F32 = jnp.float32
BF16 = jnp.bfloat16
HI = lax.Precision.HIGHEST


def _dot(a, b, precision=None):
    return jnp.dot(a, b, preferred_element_type=F32, precision=precision)


def _dot_nt(a, b, precision=None):
    return lax.dot_general(a, b, (((1,), (1,)), ((), ())),
                           preferred_element_type=F32, precision=precision)


def _dot_tn(a, b, precision=None):
    return lax.dot_general(a, b, (((0,), (0,)), ((), ())),
                           preferred_element_type=F32, precision=precision)


def _rms(x, gain):
    return x * lax.rsqrt(jnp.mean(x * x, axis=-1, keepdims=True) + NORM_EPS) * gain


def _tril_mask(n, k=0):
    r = lax.broadcasted_iota(jnp.int32, (n, n), 0)
    c = lax.broadcasted_iota(jnp.int32, (n, n), 1)
    return c <= r + k


def _ffn_body(x_ref, gpre_ref, gpost_ref, wg_ref, wu_ref, wo_ref, o_ref, h_ref, acc_ref):
    j = pl.program_id(1)

    @pl.when(j == 0)
    def _():
        h_ref[...] = _rms(x_ref[...], gpre_ref[...]).astype(BF16)

    h = h_ref[...]
    gate = _dot(h, wg_ref[...])
    up = _dot(h, wu_ref[...])
    act = (jax.nn.silu(gate) * up).astype(BF16)
    part = _dot(act, wo_ref[...])

    @pl.when(j == 0)
    def _():
        acc_ref[...] = part

    @pl.when(j > 0)
    def _():
        acc_ref[...] += part

    @pl.when(j == pl.num_programs(1) - 1)
    def _():
        o_ref[...] = x_ref[...] + 0.5 * _rms(acc_ref[...], gpost_ref[...])


def _ffn_call(x2, gpre, gpost, w_in, w_out, *, tm, tf):
    n, d = x2.shape
    dff = w_out.shape[0]
    nf = dff // tf
    return pl.pallas_call(
        _ffn_body,
        grid=(n // tm, nf),
        in_specs=[
            pl.BlockSpec((tm, d), lambda i, j: (i, 0)),
            pl.BlockSpec((1, d), lambda i, j: (0, 0)),
            pl.BlockSpec((1, d), lambda i, j: (0, 0)),
            pl.BlockSpec((d, tf), lambda i, j: (0, j)),
            pl.BlockSpec((d, tf), lambda i, j: (0, j + nf)),
            pl.BlockSpec((tf, d), lambda i, j: (j, 0)),
        ],
        out_specs=pl.BlockSpec((tm, d), lambda i, j: (i, 0)),
        out_shape=jax.ShapeDtypeStruct((n, d), F32),
        scratch_shapes=[pltpu.VMEM((tm, d), BF16), pltpu.VMEM((tm, d), F32)],
        compiler_params=pltpu.CompilerParams(
            dimension_semantics=("parallel", "arbitrary"),
            vmem_limit_bytes=V7X_VMEM_LIMIT_BYTES),
        name="ffn_sublayer",
    )(x2, gpre, gpost, w_in, w_in, w_out)


def _hgrn2_body(layer, x_ref, gpre_ref, gpost_ref, win_ref, lbraw_ref, ngain_ref, wout_ref,
                o_ref, proj_ref, y_ref, st_ref):
    tt, d = x_ref.shape[1], x_ref.shape[2]
    heads = d // HEAD_DIM

    @pl.when(pl.program_id(1) == 0)
    def _():
        st_ref[...] = jnp.zeros_like(st_ref)

    h = _rms(x_ref[0], gpre_ref[...]).astype(BF16)
    proj_ref[...] = _dot(h, win_ref[...])

    raw = lbraw_ref[...]
    e = jnp.exp(raw - jnp.max(raw, axis=0, keepdims=True))
    p = e / jnp.sum(e, axis=0, keepdims=True)
    first = p[0:1, :]
    cum = first
    for l in range(1, layer + 1):
        cum = cum + p[l:l + 1, :]
    lb = cum - first

    tril = _tril_mask(CHUNK).astype(F32)
    causal = _tril_mask(CHUNK)
    ngain = ngain_ref[...]

    def chunk_body(c, carry):
        rows = pl.ds(pl.multiple_of(c * CHUNK, CHUNK), CHUNK)
        q = jax.nn.silu(proj_ref[rows, pl.ds(0, d)])
        forget = lb + (1.0 - lb) * jax.nn.sigmoid(proj_ref[rows, pl.ds(d, d)])
        k = 1.0 - forget
        b = _dot(tril, jnp.log(forget), precision=HI)
        b_mid = b[CHUNK // 2:CHUNK // 2 + 1, :]
        b_last = b[CHUNK - 1:CHUNK, :]
        q_intra = q * jnp.exp(b - b_mid)
        k_intra = k * jnp.exp(b_mid - b)
        q_inter = q * jnp.exp(b)
        k_state = k * jnp.exp(b_last - b)
        decay = jnp.exp(b_last)
        for hh in range(heads):
            cs = slice(hh * HEAD_DIM, (hh + 1) * HEAD_DIM)
            lanes = pl.ds(hh * HEAD_DIM, HEAD_DIM)
            v = proj_ref[rows, pl.ds(2 * d + hh * HEAD_DIM, HEAD_DIM)]
            scores = _dot_nt(q_intra[:, cs], k_intra[:, cs])
            st = st_ref[hh]
            o = _dot(jnp.where(causal, scores, 0.0), v) + _dot_nt(q_inter[:, cs], st)
            st_ref[hh] = st * decay[:, cs] + _dot_tn(v, k_state[:, cs])
            gate = proj_ref[rows, pl.ds(3 * d + hh * HEAD_DIM, HEAD_DIM)]
            y = o * lax.rsqrt(jnp.mean(o * o, axis=-1, keepdims=True) + NORM_EPS)
            y_ref[rows, lanes] = (y * ngain * jax.nn.silu(gate)).astype(BF16)
        return carry

    lax.fori_loop(0, tt // CHUNK, chunk_body, 0)
    m = _dot(y_ref[...], wout_ref[...])
    o_ref[0] = x_ref[0] + _rms(m, gpost_ref[...])


def _hgrn2_call(x, gpre, gpost, w_in, lb_raw, ngain, w_out, *, layer, tt):
    bsz, t, d = x.shape
    heads = d // HEAD_DIM
    nl = lb_raw.shape[0]
    const = lambda b, i: (0, 0)
    return pl.pallas_call(
        functools.partial(_hgrn2_body, layer),
        grid=(bsz, t // tt),
        in_specs=[
            pl.BlockSpec((1, tt, d), lambda b, i: (b, i, 0)),
            pl.BlockSpec((1, d), const),
            pl.BlockSpec((1, d), const),
            pl.BlockSpec((d, 4 * d), const),
            pl.BlockSpec((nl, d), const),
            pl.BlockSpec((1, HEAD_DIM), const),
            pl.BlockSpec((d, d), const),
        ],
        out_specs=pl.BlockSpec((1, tt, d), lambda b, i: (b, i, 0)),
        out_shape=jax.ShapeDtypeStruct((bsz, t, d), F32),
        scratch_shapes=[
            pltpu.VMEM((tt, 4 * d), F32),
            pltpu.VMEM((tt, d), BF16),
            pltpu.VMEM((heads, HEAD_DIM, HEAD_DIM), F32),
        ],
        compiler_params=pltpu.CompilerParams(
            dimension_semantics=("parallel", "arbitrary"),
            vmem_limit_bytes=V7X_VMEM_LIMIT_BYTES),
        name="hgrn2_sublayer",
    )(x, gpre, gpost, w_in, lb_raw, ngain, w_out)


def _packed_matmul(xs, ys, left_half):
    out = []
    for x, y in zip(xs, ys):
        yb = y.astype(BF16)
        zero = jnp.zeros_like(yb)
        block_diag = jnp.concatenate([jnp.where(left_half, yb, zero), jnp.where(left_half, zero, yb)], axis=0)
        out.append(_dot(x.astype(BF16), block_diag))
    return out


def _unit_lower_inverse_minus_eye(a_list, diag_blocks, left_half):
    mm = functools.partial(_packed_matmul, left_half=left_half)
    dg = [jnp.where(diag_blocks, a, 0.0) for a in a_list]
    off = [a - d for a, d in zip(a_list, dg)]
    p = [-d for d in dg]
    q = p
    for _ in range(INV_BLOCK.bit_length() - 2):
        p = mm(p, p)
        pq = mm(p, q)
        q = [qi + pi + pqi for qi, pi, pqi in zip(q, p, pq)]
    x = [-(o + qo) for o, qo in zip(off, mm(q, off))]
    x2 = mm(x, x)
    y = [xi + x2i + x3i for xi, x2i, x3i in zip(x, x2, mm(x, x2))]
    return [qi + yi + yqi for qi, yi, yqi in zip(q, y, mm(y, q))]


def _gdn_body(x_ref, gpre_ref, gpost_ref, wmain_ref, wba_ref, convw_ref, alog_ref, dtb_ref,
              ngain_ref, wout_ref, o_ref, ext_ref, qkv_ref, z_ref, y_ref, s_ref,
              gcol_ref, grow_ref, beta_ref, u_ref, lhs_a_ref, lhs_b_ref):
    tt, d = x_ref.shape[1], x_ref.shape[2]
    vheads = s_ref.shape[0]
    key = (ext_ref.shape[1] - vheads * HEAD_DIM) // 2
    qk_heads = key // HEAD_DIM
    assert vheads == 2 * qk_heads
    nconv = ext_ref.shape[1]
    nchunk = tt // CHUNK
    t_idx = pl.program_id(1)

    @pl.when(t_idx == 0)
    def _():
        s_ref[...] = jnp.zeros_like(s_ref)
        ext_ref[pl.ds(0, CONV_HIST), :] = jnp.zeros((CONV_HIST, nconv), F32)

    @pl.when(t_idx > 0)
    def _():
        ext_ref[pl.ds(0, CONV_HIST), :] = ext_ref[pl.ds(tt, CONV_HIST), :]

    h = _rms(x_ref[0], gpre_ref[...]).astype(BF16)
    pm = _dot(h, wmain_ref[...])
    ext_ref[pl.ds(CONV_HIST, tt), :] = pm[:, :nconv]
    z_ref[...] = pm[:, nconv:]
    pba = _dot(h, wba_ref[...])
    beta_ref[...] = jax.nn.sigmoid(pba[:, :vheads])
    g = -jnp.exp(alog_ref[...]) * jax.nn.softplus(pba[:, vheads:] + dtb_ref[...])

    acc = convw_ref[0:1, :] * ext_ref[pl.ds(CONV_HIST - CONV_K + 1, tt), :]
    for j in range(1, CONV_K):
        acc = acc + convw_ref[j:j + 1, :] * ext_ref[pl.ds(CONV_HIST - CONV_K + 1 + j, tt), :]
    qkv_ref[...] = jax.nn.silu(acc)
    for hq in range(2 * qk_heads):
        lanes = pl.ds(hq * HEAD_DIM, HEAD_DIM)
        a = qkv_ref[:, lanes]
        a = a * lax.rsqrt(jnp.sum(a * a, axis=-1, keepdims=True) + NORM_EPS)
        if hq < qk_heads:
            a = a * (HEAD_DIM ** -0.5)
        qkv_ref[:, lanes] = a

    tril = _tril_mask(CHUNK).astype(F32)
    r16 = lax.broadcasted_iota(jnp.int32, (vheads, vheads), 0)
    c16 = lax.broadcasted_iota(jnp.int32, (vheads, vheads), 1)
    eye_h = (r16 == c16).astype(F32)
    for c in range(nchunk):
        gc = g[c * CHUNK:(c + 1) * CHUNK, :]
        gcol_ref[pl.ds(c * CHUNK, CHUNK), :] = _dot(tril, gc, precision=HI)
        g_t = _dot_nt(eye_h, gc, precision=HI)
        grow_ref[c] = _dot_nt(g_t, tril, precision=HI)

    row = lax.broadcasted_iota(jnp.int32, (CHUNK, 2 * CHUNK), 0)
    lane = lax.broadcasted_iota(jnp.int32, (CHUNK, 2 * CHUNK), 1)
    col = lane % CHUNK
    left_half = lane < CHUNK
    incl = col <= row
    strict = col < row
    diag_blocks = (row // INV_BLOCK) == (col // INV_BLOCK)
    ngain = ngain_ref[...]

    def prepare_chunk(c, carry):
        rows = pl.ds(pl.multiple_of(c * CHUNK, CHUNK), CHUNK)
        g_cols = gcol_ref[rows, :]
        g_rows = grow_ref[c]
        betas = beta_ref[rows, :]
        a_list, ctx = [], []
        for hq in range(qk_heads):
            q = qkv_ref[rows, pl.ds(hq * HEAD_DIM, HEAD_DIM)]
            k = qkv_ref[rows, pl.ds(key + hq * HEAD_DIM, HEAD_DIM)]
            kb = k.astype(BF16)
            prods = _dot_nt(jnp.concatenate([kb, q.astype(BF16)], axis=0),
                            jnp.concatenate([kb, kb], axis=0))
            heads = (2 * hq, 2 * hq + 1)
            g_full = [jnp.broadcast_to(g_cols[:, hv:hv + 1], (CHUNK, HEAD_DIM)) for hv in heads]
            b_full = [jnp.broadcast_to(betas[:, hv:hv + 1], (CHUNK, HEAD_DIM)) for hv in heads]
            g_col = jnp.where(left_half, g_full[0], g_full[1])
            b_col = jnp.where(left_half, b_full[0], b_full[1])
            g_row = jnp.concatenate([g_rows[hv:hv + 1, :] for hv in heads], axis=1)
            gamma = jnp.exp(jnp.where(incl, g_col - g_row, -jnp.inf))
            a_list.append(jnp.where(strict, prods[:CHUNK] * gamma * b_col, 0.0))
            ctx.append((q, k, g_full, b_full, prods[CHUNK:] * gamma))
        t_list = _unit_lower_inverse_minus_eye(a_list, diag_blocks, left_half)
        for hq in range(qk_heads):
            q, k, g_full, b_full, qk_gamma = ctx[hq]
            k_t = k.T
            for half in range(2):
                hv = 2 * hq + half
                half_lanes = slice(half * CHUNK, (half + 1) * CHUNK)
                v = qkv_ref[rows, pl.ds(2 * key + hv * HEAD_DIM, HEAD_DIM)]
                exp_g = jnp.exp(g_full[half])
                rhs = jnp.concatenate([v * b_full[half], k * (b_full[half] * exp_g)], axis=1)
                uw = rhs + _dot(t_list[hq][:, half_lanes].astype(BF16), rhs.astype(BF16))
                u_ref[c, hv] = uw[:, :HEAD_DIM]
                lhs_a_ref[c, hv, pl.ds(0, CHUNK), :] = uw[:, HEAD_DIM:].astype(BF16)
                lhs_a_ref[c, hv, pl.ds(CHUNK, CHUNK), :] = (q * exp_g).astype(BF16)
                g_row = g_rows[hv:hv + 1, :]
                g_last = g_row[:, CHUNK - 1:CHUNK]
                lhs_b_ref[c, hv, pl.ds(0, CHUNK), :] = qk_gamma[:, half_lanes].astype(BF16)
                lhs_b_ref[c, hv, pl.ds(CHUNK, HEAD_DIM), :] = (k_t * jnp.exp(g_last - g_row)).astype(BF16)
        return carry

    def recur_chunk(c, carry):
        rows = pl.ds(pl.multiple_of(c * CHUNK, CHUNK), CHUNK)
        decays = jnp.exp(grow_ref[c][:, CHUNK - 1:CHUNK])
        states = [s_ref[hv] for hv in range(vheads)]
        ws_qs = [_dot(lhs_a_ref[c, hv], states[hv].astype(BF16)) for hv in range(vheads)]
        for hv in range(vheads):
            v_new = u_ref[c, hv] - ws_qs[hv][:CHUNK]
            upd = _dot(lhs_b_ref[c, hv], v_new.astype(BF16))
            o = ws_qs[hv][CHUNK:] + upd[:CHUNK]
            s_ref[hv] = decays[hv:hv + 1, :] * states[hv] + upd[CHUNK:]
            zg = z_ref[rows, pl.ds(hv * HEAD_DIM, HEAD_DIM)]
            y = o * lax.rsqrt(jnp.mean(o * o, axis=-1, keepdims=True) + NORM_EPS)
            y_ref[rows, pl.ds(hv * HEAD_DIM, HEAD_DIM)] = (y * ngain * jax.nn.silu(zg)).astype(BF16)
        return carry

    lax.fori_loop(0, nchunk, prepare_chunk, 0)
    lax.fori_loop(0, nchunk, recur_chunk, 0)
    m = _dot(y_ref[...], wout_ref[...])
    o_ref[0] = x_ref[0] + _rms(m, gpost_ref[...])


def _gdn_call(x, gpre, gpost, w_main, w_ba, conv_w, a_log, dt_bias, ngain, w_out, *, tt):
    bsz, t, d = x.shape
    vheads = a_log.shape[1]
    nconv = conv_w.shape[1]
    val = vheads * HEAD_DIM
    nchunk = tt // CHUNK
    const = lambda b, i: (0, 0)
    resident = dict(pipeline_mode=pl.Buffered(1))
    return pl.pallas_call(
        _gdn_body,
        grid=(bsz, t // tt),
        in_specs=[
            pl.BlockSpec((1, tt, d), lambda b, i: (b, i, 0)),
            pl.BlockSpec((1, d), const),
            pl.BlockSpec((1, d), const),
            pl.BlockSpec((d, nconv + val), const, **resident),
            pl.BlockSpec((d, 2 * vheads), const),
            pl.BlockSpec((CONV_K, nconv), const),
            pl.BlockSpec((1, vheads), const),
            pl.BlockSpec((1, vheads), const),
            pl.BlockSpec((1, HEAD_DIM), const),
            pl.BlockSpec((val, d), const, **resident),
        ],
        out_specs=pl.BlockSpec((1, tt, d), lambda b, i: (b, i, 0)),
        out_shape=jax.ShapeDtypeStruct((bsz, t, d), F32),
        scratch_shapes=[
            pltpu.VMEM((tt + CONV_HIST, nconv), F32),
            pltpu.VMEM((tt, nconv), F32),
            pltpu.VMEM((tt, val), F32),
            pltpu.VMEM((tt, val), BF16),
            pltpu.VMEM((vheads, HEAD_DIM, HEAD_DIM), F32),
            pltpu.VMEM((tt, vheads), F32),
            pltpu.VMEM((nchunk, vheads, CHUNK), F32),
            pltpu.VMEM((tt, vheads), F32),
            pltpu.VMEM((nchunk, vheads, CHUNK, HEAD_DIM), F32),
            pltpu.VMEM((nchunk, vheads, 2 * CHUNK, HEAD_DIM), BF16),
            pltpu.VMEM((nchunk, vheads, CHUNK + HEAD_DIM, CHUNK), BF16),
        ],
        compiler_params=pltpu.CompilerParams(
            dimension_semantics=("parallel", "arbitrary"),
            vmem_limit_bytes=V7X_VMEM_LIMIT_BYTES),
        name="gdn_sublayer",
    )(x, gpre, gpost, w_main, w_ba, conv_w, a_log, dt_bias, ngain, w_out)


def _tile(n, pref):
    t = min(n, pref)
    assert n % t == 0, (n, t)
    return t


def kernel(x, norm_gains, ffn1_w_in, ffn1_w_out, ffn2_w_in, ffn2_w_out, hg_w_in, hg_lower_bounds,
           hg_norm_gain, hg_w_out, gd_w_in, gd_conv_w, gd_a_log, gd_dt_bias, gd_norm_gain, gd_w_out):
    bsz, t, d = x.shape
    depth = norm_gains.shape[0]
    n_mixers = 2
    dff = ffn1_w_out.shape[1]
    nconv = gd_conv_w.shape[2]
    vheads = gd_a_log.shape[1]
    n_main = nconv + vheads * HEAD_DIM

    tm = _tile(bsz * t, 1024)
    tf = 256 if dff % 256 == 0 else 128
    tt_hg = _tile(t, 512)
    tt_gd = _tile(t, 256)

    def row(v):
        return v.reshape(1, -1)

    def ffn(xc, gpre, gpost, w_in, w_out):
        y = _ffn_call(xc.reshape(bsz * t, d), row(gpre), row(gpost),
                      w_in.astype(BF16), w_out.astype(BF16), tm=tm, tf=tf)
        return y.reshape(bsz, t, d)

    for i in range(depth):
        ng = norm_gains[i]
        j = i // n_mixers
        x = ffn(x, ng[0], ng[1], ffn1_w_in[i], ffn1_w_out[i])
        if i % n_mixers == 0:
            x = _hgrn2_call(x, row(ng[2]), row(ng[3]), hg_w_in[j].astype(BF16), hg_lower_bounds,
                            row(hg_norm_gain[j]), hg_w_out[j].astype(BF16), layer=j, tt=tt_hg)
        else:
            w_in = gd_w_in[j].astype(BF16)
            x = _gdn_call(x, row(ng[2]), row(ng[3]), w_in[:, :n_main], w_in[:, n_main:],
                          gd_conv_w[j], row(gd_a_log[j]), row(gd_dt_bias[j]),
                          row(gd_norm_gain[j]), gd_w_out[j].astype(BF16), tt=tt_gd)
        x = ffn(x, ng[4], ng[5], ffn2_w_in[i], ffn2_w_out[i])
    return x
```

```python
import functools

import jax
import jax.numpy as jnp
from jax import lax
from jax.experimental import pallas as pl
from jax.experimental.pallas import tpu as pltpu

NORM_EPS = 1e-6
CHUNK = 64
HEAD_DIM = 128
CONV_K = 4
CONV_HIST = 8
INV_BLOCK = 16
PREP_CHUNKS = 4
V7X_VMEM_LIMIT_BYTES = 58 * 1024 * 1024

F32 = jnp.float32
BF16 = jnp.bfloat16
HI = lax.Precision.HIGHEST


def _dot(a, b, precision=None):
    return jnp.dot(a, b, preferred_element_type=F32, precision=precision)


def _dot_nt(a, b, precision=None):
    return lax.dot_general(a, b, (((1,), (1,)), ((), ())),
                           preferred_element_type=F32, precision=precision)


def _dot_tn(a, b, precision=None):
    return lax.dot_general(a, b, (((0,), (0,)), ((), ())),
                           preferred_element_type=F32, precision=precision)


def _rms(x, gain):
    return x * lax.rsqrt(jnp.mean(x * x, axis=-1, keepdims=True) + NORM_EPS) * gain


def _tril_mask(n, k=0):
    r = lax.broadcasted_iota(jnp.int32, (n, n), 0)
    c = lax.broadcasted_iota(jnp.int32, (n, n), 1)
    return c <= r + k


def _ffn_body(x_ref, gpre_ref, gpost_ref, wg_ref, wu_ref, wo_ref, o_ref, h_ref, acc_ref):
    j = pl.program_id(1)

    @pl.when(j == 0)
    def _():
        h_ref[...] = _rms(x_ref[...], gpre_ref[...]).astype(BF16)
        acc_ref[...] = jnp.zeros_like(acc_ref)

    h = h_ref[...]
    gate = _dot(h, wg_ref[...])
    up = _dot(h, wu_ref[...])
    act = (jax.nn.silu(gate) * up).astype(BF16)
    acc_ref[...] += _dot(act, wo_ref[...])

    @pl.when(j == pl.num_programs(1) - 1)
    def _():
        o_ref[...] = x_ref[...] + 0.5 * _rms(acc_ref[...], gpost_ref[...])


def _ffn_call(x2, gpre, gpost, w_in, w_out, *, tm, tf):
    n, d = x2.shape
    dff = w_out.shape[0]
    nf = dff // tf
    return pl.pallas_call(
        _ffn_body,
        grid=(n // tm, nf),
        in_specs=[
            pl.BlockSpec((tm, d), lambda i, j: (i, 0)),
            pl.BlockSpec((1, d), lambda i, j: (0, 0)),
            pl.BlockSpec((1, d), lambda i, j: (0, 0)),
            pl.BlockSpec((d, tf), lambda i, j: (0, j)),
            pl.BlockSpec((d, tf), lambda i, j: (0, j + nf)),
            pl.BlockSpec((tf, d), lambda i, j: (j, 0)),
        ],
        out_specs=pl.BlockSpec((tm, d), lambda i, j: (i, 0)),
        out_shape=jax.ShapeDtypeStruct((n, d), F32),
        scratch_shapes=[pltpu.VMEM((tm, d), BF16), pltpu.VMEM((tm, d), F32)],
        compiler_params=pltpu.CompilerParams(
            dimension_semantics=("parallel", "arbitrary"),
            vmem_limit_bytes=V7X_VMEM_LIMIT_BYTES),
        name="ffn_sublayer",
    )(x2, gpre, gpost, w_in, w_in, w_out)


def _hgrn2_body(layer, x_ref, gpre_ref, gpost_ref, win_ref, lbraw_ref, ngain_ref, wout_ref,
                o_ref, proj_ref, y_ref, st_ref):
    tt, d = x_ref.shape[1], x_ref.shape[2]
    heads = d // HEAD_DIM

    @pl.when(pl.program_id(1) == 0)
    def _():
        st_ref[...] = jnp.zeros_like(st_ref)

    h = _rms(x_ref[0], gpre_ref[...]).astype(BF16)
    proj_ref[...] = _dot(h, win_ref[...])

    raw = lbraw_ref[...]
    e = jnp.exp(raw - jnp.max(raw, axis=0, keepdims=True))
    p = e / jnp.sum(e, axis=0, keepdims=True)
    first = p[0:1, :]
    cum = first
    for l in range(1, layer + 1):
        cum = cum + p[l:l + 1, :]
    lb = cum - first

    tril = _tril_mask(CHUNK).astype(F32)
    causal = _tril_mask(CHUNK)
    ngain = ngain_ref[...]

    def chunk_body(c, carry):
        rows = pl.ds(pl.multiple_of(c * CHUNK, CHUNK), CHUNK)
        q = jax.nn.silu(proj_ref[rows, pl.ds(0, d)])
        forget = lb + (1.0 - lb) * jax.nn.sigmoid(proj_ref[rows, pl.ds(d, d)])
        k = 1.0 - forget
        b = _dot(tril, jnp.log(forget), precision=HI)
        b_mid = b[CHUNK // 2:CHUNK // 2 + 1, :]
        b_last = b[CHUNK - 1:CHUNK, :]
        q_intra = q * jnp.exp(b - b_mid)
        k_intra = k * jnp.exp(b_mid - b)
        q_inter = q * jnp.exp(b)
        k_state = k * jnp.exp(b_last - b)
        decay = jnp.exp(b_last)
        cols = [slice(hh * HEAD_DIM, (hh + 1) * HEAD_DIM) for hh in range(heads)]
        vs = [proj_ref[rows, pl.ds(2 * d + hh * HEAD_DIM, HEAD_DIM)].astype(BF16) for hh in range(heads)]
        sts = [st_ref[hh] for hh in range(heads)]
        q_intra, k_intra = q_intra.astype(BF16), k_intra.astype(BF16)
        q_inter, k_state = q_inter.astype(BF16), k_state.astype(BF16)
        scores = [_dot_nt(q_intra[:, cs], k_intra[:, cs]) for cs in cols]
        inter = [_dot_nt(q_inter[:, cs], st.astype(BF16)) for cs, st in zip(cols, sts)]
        update = [_dot_tn(v, k_state[:, cs]) for cs, v in zip(cols, vs)]
        for hh in range(heads):
            cs = cols[hh]
            o = _dot(jnp.where(causal, scores[hh], 0.0).astype(BF16), vs[hh]) + inter[hh]
            st_ref[hh] = sts[hh] * decay[:, cs] + update[hh]
            gate = proj_ref[rows, pl.ds(3 * d + hh * HEAD_DIM, HEAD_DIM)]
            y = o * lax.rsqrt(jnp.mean(o * o, axis=-1, keepdims=True) + NORM_EPS)
            y_ref[rows, pl.ds(hh * HEAD_DIM, HEAD_DIM)] = (y * ngain * jax.nn.silu(gate)).astype(BF16)
        return carry

    lax.fori_loop(0, tt // CHUNK, chunk_body, 0)
    m = _dot(y_ref[...], wout_ref[...])
    o_ref[0] = x_ref[0] + _rms(m, gpost_ref[...])


def _hgrn2_call(x, gpre, gpost, w_in, lb_raw, ngain, w_out, *, layer, tt):
    bsz, t, d = x.shape
    heads = d // HEAD_DIM
    nl = lb_raw.shape[0]
    const = lambda b, i: (0, 0)
    return pl.pallas_call(
        functools.partial(_hgrn2_body, layer),
        grid=(bsz, t // tt),
        in_specs=[
            pl.BlockSpec((1, tt, d), lambda b, i: (b, i, 0)),
            pl.BlockSpec((1, d), const),
            pl.BlockSpec((1, d), const),
            pl.BlockSpec((d, 4 * d), const),
            pl.BlockSpec((nl, d), const),
            pl.BlockSpec((1, HEAD_DIM), const),
            pl.BlockSpec((d, d), const),
        ],
        out_specs=pl.BlockSpec((1, tt, d), lambda b, i: (b, i, 0)),
        out_shape=jax.ShapeDtypeStruct((bsz, t, d), F32),
        scratch_shapes=[
            pltpu.VMEM((tt, 4 * d), F32),
            pltpu.VMEM((tt, d), BF16),
            pltpu.VMEM((heads, HEAD_DIM, HEAD_DIM), F32),
        ],
        compiler_params=pltpu.CompilerParams(
            dimension_semantics=("parallel", "arbitrary"),
            vmem_limit_bytes=V7X_VMEM_LIMIT_BYTES),
        name="hgrn2_sublayer",
    )(x, gpre, gpost, w_in, lb_raw, ngain, w_out)


def _packed_matmul(xs, ys, left_half):
    out = []
    for x, y in zip(xs, ys):
        yb = y.astype(BF16)
        zero = jnp.zeros_like(yb)
        block_diag = jnp.concatenate([jnp.where(left_half, yb, zero), jnp.where(left_half, zero, yb)], axis=0)
        out.append(_dot(x.astype(BF16), block_diag))
    return out


def _unit_lower_inverse_minus_eye(a_list, diag_blocks, left_half):
    mm = functools.partial(_packed_matmul, left_half=left_half)
    dg = [jnp.where(diag_blocks, a, 0.0) for a in a_list]
    off = [a - d for a, d in zip(a_list, dg)]
    n = len(a_list)
    q = [-d for d in dg]
    p = mm(q, q)
    steps = INV_BLOCK.bit_length() - 2
    for step in range(steps):
        last = step == steps - 1
        prods = mm(p if last else p + p, q if last else q + p)
        q = [qi + pi + pqi for qi, pi, pqi in zip(q, p, prods[:n])]
        p = prods[n:]
    x = [-(o + qo) for o, qo in zip(off, mm(q, off))]
    x2 = mm(x, x)
    y = [xi + x2i + x3i for xi, x2i, x3i in zip(x, x2, mm(x, x2))]
    return [qi + yi + yqi for qi, yi, yqi in zip(q, y, mm(y, q))]


def _gdn_body(x_ref, gpre_ref, gpost_ref, wmain_ref, wba_ref, convw_ref, alog_ref, dtb_ref,
              ngain_ref, wout_ref, o_ref, ext_ref, qkv_ref, z_ref, y_ref, s_ref,
              gcol_ref, grow_ref, beta_ref, u_ref, lhs_a_ref, lhs_b_ref):
    tt, d = x_ref.shape[1], x_ref.shape[2]
    vheads = s_ref.shape[0]
    key = (ext_ref.shape[1] - vheads * HEAD_DIM) // 2
    qk_heads = key // HEAD_DIM
    assert vheads == 2 * qk_heads
    nconv = ext_ref.shape[1]
    nchunk = tt // CHUNK
    t_idx = pl.program_id(1)

    @pl.when(t_idx == 0)
    def _():
        s_ref[...] = jnp.zeros_like(s_ref)
        ext_ref[pl.ds(0, CONV_HIST), :] = jnp.zeros((CONV_HIST, nconv), F32)

    @pl.when(t_idx > 0)
    def _():
        ext_ref[pl.ds(0, CONV_HIST), :] = ext_ref[pl.ds(tt, CONV_HIST), :]

    h = _rms(x_ref[0], gpre_ref[...]).astype(BF16)
    pm = _dot(h, wmain_ref[...])
    ext_ref[pl.ds(CONV_HIST, tt), :] = pm[:, :nconv]
    z_ref[...] = pm[:, nconv:]
    pba = _dot(h, wba_ref[...])
    beta_ref[...] = jax.nn.sigmoid(pba[:, :vheads])
    g = -jnp.exp(alog_ref[...]) * jax.nn.softplus(pba[:, vheads:] + dtb_ref[...])

    acc = convw_ref[0:1, :] * ext_ref[pl.ds(CONV_HIST - CONV_K + 1, tt), :]
    for j in range(1, CONV_K):
        acc = acc + convw_ref[j:j + 1, :] * ext_ref[pl.ds(CONV_HIST - CONV_K + 1 + j, tt), :]
    qkv_ref[...] = jax.nn.silu(acc)
    for hq in range(2 * qk_heads):
        lanes = pl.ds(hq * HEAD_DIM, HEAD_DIM)
        a = qkv_ref[:, lanes]
        a = a * lax.rsqrt(jnp.sum(a * a, axis=-1, keepdims=True) + NORM_EPS)
        if hq < qk_heads:
            a = a * (HEAD_DIM ** -0.5)
        qkv_ref[:, lanes] = a

    tril = _tril_mask(CHUNK).astype(F32)
    r16 = lax.broadcasted_iota(jnp.int32, (vheads, vheads), 0)
    c16 = lax.broadcasted_iota(jnp.int32, (vheads, vheads), 1)
    eye_h = (r16 == c16).astype(F32)
    for c in range(nchunk):
        gc = g[c * CHUNK:(c + 1) * CHUNK, :]
        gcol_ref[pl.ds(c * CHUNK, CHUNK), :] = _dot(tril, gc, precision=HI)
        g_t = _dot_nt(eye_h, gc, precision=HI)
        grow_ref[c] = _dot_nt(g_t, tril, precision=HI)

    row = lax.broadcasted_iota(jnp.int32, (CHUNK, 2 * CHUNK), 0)
    lane = lax.broadcasted_iota(jnp.int32, (CHUNK, 2 * CHUNK), 1)
    col = lane % CHUNK
    left_half = lane < CHUNK
    incl = col <= row
    strict = col < row
    diag_blocks = (row // INV_BLOCK) == (col // INV_BLOCK)
    ngain = ngain_ref[...]

    def prepare_chunks(ci, carry):
        a_list, ctx = [], []
        for cc in range(PREP_CHUNKS):
            c = ci * PREP_CHUNKS + cc
            rows = pl.ds(pl.multiple_of(c * CHUNK, CHUNK), CHUNK)
            g_cols = gcol_ref[rows, :]
            g_rows = grow_ref[c]
            betas = beta_ref[rows, :]
            for hq in range(qk_heads):
                q = qkv_ref[rows, pl.ds(hq * HEAD_DIM, HEAD_DIM)]
                k = qkv_ref[rows, pl.ds(key + hq * HEAD_DIM, HEAD_DIM)]
                kb = k.astype(BF16)
                prods = _dot_nt(jnp.concatenate([kb, q.astype(BF16)], axis=0),
                                jnp.concatenate([kb, kb], axis=0))
                heads = (2 * hq, 2 * hq + 1)
                g_full = [jnp.broadcast_to(g_cols[:, hv:hv + 1], (CHUNK, HEAD_DIM)) for hv in heads]
                b_full = [jnp.broadcast_to(betas[:, hv:hv + 1], (CHUNK, HEAD_DIM)) for hv in heads]
                g_col = jnp.where(left_half, g_full[0], g_full[1])
                b_col = jnp.where(left_half, b_full[0], b_full[1])
                g_row = jnp.concatenate([g_rows[hv:hv + 1, :] for hv in heads], axis=1)
                gamma = jnp.exp(jnp.where(incl, g_col - g_row, -jnp.inf))
                a_list.append(jnp.where(strict, prods[:CHUNK] * gamma * b_col, 0.0))
                ctx.append((c, rows, hq, q, k, g_rows, g_full, b_full, prods[CHUNK:] * gamma))
        t_list = _unit_lower_inverse_minus_eye(a_list, diag_blocks, left_half)
        for t_m, (c, rows, hq, q, k, g_rows, g_full, b_full, qk_gamma) in zip(t_list, ctx):
            k_t = k.T
            for half in range(2):
                hv = 2 * hq + half
                half_lanes = slice(half * CHUNK, (half + 1) * CHUNK)
                v = qkv_ref[rows, pl.ds(2 * key + hv * HEAD_DIM, HEAD_DIM)]
                exp_g = jnp.exp(g_full[half])
                rhs = jnp.concatenate([v * b_full[half], k * (b_full[half] * exp_g)], axis=1)
                uw = rhs + _dot(t_m[:, half_lanes].astype(BF16), rhs.astype(BF16))
                u_ref[c, hv] = uw[:, :HEAD_DIM]
                lhs_a_ref[c, hv, pl.ds(0, CHUNK), :] = uw[:, HEAD_DIM:].astype(BF16)
                lhs_a_ref[c, hv, pl.ds(CHUNK, CHUNK), :] = (q * exp_g).astype(BF16)
                g_row = g_rows[hv:hv + 1, :]
                g_last = g_row[:, CHUNK - 1:CHUNK]
                lhs_b_ref[c, hv, pl.ds(0, CHUNK), :] = qk_gamma[:, half_lanes].astype(BF16)
                lhs_b_ref[c, hv, pl.ds(CHUNK, HEAD_DIM), :] = (k_t * jnp.exp(g_last - g_row)).astype(BF16)
        return carry

    def recur_chunk(c, carry):
        rows = pl.ds(pl.multiple_of(c * CHUNK, CHUNK), CHUNK)
        decays = jnp.exp(grow_ref[c][:, CHUNK - 1:CHUNK])
        states = [s_ref[hv] for hv in range(vheads)]
        ws_qs = [_dot(lhs_a_ref[c, hv], states[hv].astype(BF16)) for hv in range(vheads)]
        for hv in range(vheads):
            v_new = u_ref[c, hv] - ws_qs[hv][:CHUNK]
            upd = _dot(lhs_b_ref[c, hv], v_new.astype(BF16))
            o = ws_qs[hv][CHUNK:] + upd[:CHUNK]
            s_ref[hv] = decays[hv:hv + 1, :] * states[hv] + upd[CHUNK:]
            zg = z_ref[rows, pl.ds(hv * HEAD_DIM, HEAD_DIM)]
            y = o * lax.rsqrt(jnp.mean(o * o, axis=-1, keepdims=True) + NORM_EPS)
            y_ref[rows, pl.ds(hv * HEAD_DIM, HEAD_DIM)] = (y * ngain * jax.nn.silu(zg)).astype(BF16)
        return carry

    lax.fori_loop(0, nchunk // PREP_CHUNKS, prepare_chunks, 0)
    lax.fori_loop(0, nchunk, recur_chunk, 0)
    m = _dot(y_ref[...], wout_ref[...])
    o_ref[0] = x_ref[0] + _rms(m, gpost_ref[...])


def _gdn_call(x, gpre, gpost, w_main, w_ba, conv_w, a_log, dt_bias, ngain, w_out, *, tt):
    bsz, t, d = x.shape
    vheads = a_log.shape[1]
    nconv = conv_w.shape[1]
    val = vheads * HEAD_DIM
    nchunk = tt // CHUNK
    const = lambda b, i: (0, 0)
    resident = dict(pipeline_mode=pl.Buffered(1))
    return pl.pallas_call(
        _gdn_body,
        grid=(bsz, t // tt),
        in_specs=[
            pl.BlockSpec((1, tt, d), lambda b, i: (b, i, 0)),
            pl.BlockSpec((1, d), const),
            pl.BlockSpec((1, d), const),
            pl.BlockSpec((d, nconv + val), const, **resident),
            pl.BlockSpec((d, 2 * vheads), const),
            pl.BlockSpec((CONV_K, nconv), const),
            pl.BlockSpec((1, vheads), const),
            pl.BlockSpec((1, vheads), const),
            pl.BlockSpec((1, HEAD_DIM), const),
            pl.BlockSpec((val, d), const, **resident),
        ],
        out_specs=pl.BlockSpec((1, tt, d), lambda b, i: (b, i, 0)),
        out_shape=jax.ShapeDtypeStruct((bsz, t, d), F32),
        scratch_shapes=[
            pltpu.VMEM((tt + CONV_HIST, nconv), F32),
            pltpu.VMEM((tt, nconv), F32),
            pltpu.VMEM((tt, val), F32),
            pltpu.VMEM((tt, val), BF16),
            pltpu.VMEM((vheads, HEAD_DIM, HEAD_DIM), F32),
            pltpu.VMEM((tt, vheads), F32),
            pltpu.VMEM((nchunk, vheads, CHUNK), F32),
            pltpu.VMEM((tt, vheads), F32),
            pltpu.VMEM((nchunk, vheads, CHUNK, HEAD_DIM), F32),
            pltpu.VMEM((nchunk, vheads, 2 * CHUNK, HEAD_DIM), BF16),
            pltpu.VMEM((nchunk, vheads, CHUNK + HEAD_DIM, CHUNK), BF16),
        ],
        compiler_params=pltpu.CompilerParams(
            dimension_semantics=("parallel", "arbitrary"),
            vmem_limit_bytes=V7X_VMEM_LIMIT_BYTES),
        name="gdn_sublayer",
    )(x, gpre, gpost, w_main, w_ba, conv_w, a_log, dt_bias, ngain, w_out)


def _tile(n, pref):
    t = min(n, pref)
    assert n % t == 0, (n, t)
    return t


def kernel(x, norm_gains, ffn1_w_in, ffn1_w_out, ffn2_w_in, ffn2_w_out, hg_w_in, hg_lower_bounds,
           hg_norm_gain, hg_w_out, gd_w_in, gd_conv_w, gd_a_log, gd_dt_bias, gd_norm_gain, gd_w_out):
    bsz, t, d = x.shape
    depth = norm_gains.shape[0]
    n_mixers = 2
    dff = ffn1_w_out.shape[1]
    nconv = gd_conv_w.shape[2]
    vheads = gd_a_log.shape[1]
    n_main = nconv + vheads * HEAD_DIM

    tm = _tile(bsz * t, 1024)
    tf = 256 if dff % 256 == 0 else 128
    tt_hg = _tile(t, 512)
    tt_gd = _tile(t, 256)

    def row(v):
        return v.reshape(1, -1)

    def ffn(xc, gpre, gpost, w_in, w_out):
        y = _ffn_call(xc.reshape(bsz * t, d), row(gpre), row(gpost),
                      w_in.astype(BF16), w_out.astype(BF16), tm=tm, tf=tf)
        return y.reshape(bsz, t, d)

    for i in range(depth):
        ng = norm_gains[i]
        j = i // n_mixers
        x = ffn(x, ng[0], ng[1], ffn1_w_in[i], ffn1_w_out[i])
        if i % n_mixers == 0:
            x = _hgrn2_call(x, row(ng[2]), row(ng[3]), hg_w_in[j].astype(BF16), hg_lower_bounds,
                            row(hg_norm_gain[j]), hg_w_out[j].astype(BF16), layer=j, tt=tt_hg)
        else:
            w_in = gd_w_in[j].astype(BF16)
            x = _gdn_call(x, row(ng[2]), row(ng[3]), w_in[:, :n_main], w_in[:, n_main:],
                          gd_conv_w[j], row(gd_a_log[j]), row(gd_dt_bias[j]),
                          row(gd_norm_gain[j]), gd_w_out[j].astype(BF16), tt=tt_gd)
        x = ffn(x, ng[4], ng[5], ffn2_w_in[i], ffn2_w_out[i])
    return x
```

```python
import functools

import jax
import jax.numpy as jnp
from jax import lax
from jax.experimental import pallas as pl
from jax.experimental.pallas import tpu as pltpu

NORM_EPS = 1e-6
CHUNK = 64
HEAD_DIM = 128
CONV_K = 4
CONV_HIST = 8
INV_BLOCK = 16
FFN_COL_BLOCK = 256
PREP_CHUNKS = 4
V7X_VMEM_LIMIT_BYTES = 58 * 1024 * 1024

F32 = jnp.float32
BF16 = jnp.bfloat16
HI = lax.Precision.HIGHEST


def _dot(a, b, precision=None):
    return jnp.dot(a, b, preferred_element_type=F32, precision=precision)


def _dot_nt(a, b, precision=None):
    return lax.dot_general(a, b, (((1,), (1,)), ((), ())),
                           preferred_element_type=F32, precision=precision)


def _dot_tn(a, b, precision=None):
    return lax.dot_general(a, b, (((0,), (0,)), ((), ())),
                           preferred_element_type=F32, precision=precision)


def _rms(x, gain):
    return x * lax.rsqrt(jnp.mean(x * x, axis=-1, keepdims=True) + NORM_EPS) * gain


def _tril_mask(n, k=0):
    r = lax.broadcasted_iota(jnp.int32, (n, n), 0)
    c = lax.broadcasted_iota(jnp.int32, (n, n), 1)
    return c <= r + k


def _ffn_body(x_ref, gpre_ref, gpost_ref, wi_ref, wo_ref, o_ref, act_ref):
    nsub, ts, dff = act_ref.shape
    d = x_ref.shape[1]
    nblk = dff // FFN_COL_BLOCK
    gpre, gpost = gpre_ref[...], gpost_ref[...]

    def pre_norm(s):
        return _rms(x_ref[pl.ds(s * ts, ts), :], gpre).astype(BF16)

    def hidden_block(s, h, jb):
        cols = slice(jb * FFN_COL_BLOCK, (jb + 1) * FFN_COL_BLOCK)
        up_cols = slice(dff + jb * FFN_COL_BLOCK, dff + (jb + 1) * FFN_COL_BLOCK)
        gate = _dot(h, wi_ref[:, cols])
        up = _dot(h, wi_ref[:, up_cols])
        act_ref[s, :, cols] = (jax.nn.silu(gate) * up).astype(BF16)

    def finish(s):
        rows = pl.ds(s * ts, ts)
        y = _dot(act_ref[s], wo_ref[...])
        o_ref[rows, :] = x_ref[rows, :] + 0.5 * _rms(y, gpost)

    h = pre_norm(0)
    for s in range(nsub):
        h_next = None
        for jb in range(nblk):
            hidden_block(s, h, jb)
            if jb == 0 and s + 1 < nsub:
                h_next = pre_norm(s + 1)
        if s > 0:
            finish(s - 1)
        h = h_next
    finish(nsub - 1)


def _ffn_call(x2, gpre, gpost, w_in, w_out, *, tm, nsub):
    n, d = x2.shape
    dff = w_out.shape[0]
    const = lambda i: (0, 0)
    resident = dict(pipeline_mode=pl.Buffered(1))
    return pl.pallas_call(
        _ffn_body,
        grid=(n // tm,),
        in_specs=[
            pl.BlockSpec((tm, d), lambda i: (i, 0)),
            pl.BlockSpec((1, d), const),
            pl.BlockSpec((1, d), const),
            pl.BlockSpec((d, 2 * dff), const, **resident),
            pl.BlockSpec((dff, d), const, **resident),
        ],
        out_specs=pl.BlockSpec((tm, d), lambda i: (i, 0)),
        out_shape=jax.ShapeDtypeStruct((n, d), F32),
        scratch_shapes=[pltpu.VMEM((nsub, tm // nsub, dff), BF16)],
        compiler_params=pltpu.CompilerParams(
            dimension_semantics=("parallel",),
            vmem_limit_bytes=V7X_VMEM_LIMIT_BYTES),
        name="ffn_sublayer",
    )(x2, gpre, gpost, w_in, w_out)


def _hgrn2_body(layer, x_ref, gpre_ref, gpost_ref, win_ref, lbraw_ref, ngain_ref, wout_ref,
                o_ref, proj_ref, y_ref, st_ref):
    tt, d = x_ref.shape[1], x_ref.shape[2]
    heads = d // HEAD_DIM

    @pl.when(pl.program_id(1) == 0)
    def _():
        st_ref[...] = jnp.zeros_like(st_ref)

    h = _rms(x_ref[0], gpre_ref[...]).astype(BF16)
    proj_ref[...] = _dot(h, win_ref[...])

    raw = lbraw_ref[...]
    e = jnp.exp(raw - jnp.max(raw, axis=0, keepdims=True))
    p = e / jnp.sum(e, axis=0, keepdims=True)
    first = p[0:1, :]
    cum = first
    for l in range(1, layer + 1):
        cum = cum + p[l:l + 1, :]
    lb = cum - first

    tril = _tril_mask(CHUNK).astype(F32)
    causal = _tril_mask(CHUNK)
    ngain = ngain_ref[...]

    def chunk_body(c, carry):
        rows = pl.ds(pl.multiple_of(c * CHUNK, CHUNK), CHUNK)
        q = jax.nn.silu(proj_ref[rows, pl.ds(0, d)])
        forget = lb + (1.0 - lb) * jax.nn.sigmoid(proj_ref[rows, pl.ds(d, d)])
        k = 1.0 - forget
        b = _dot(tril, jnp.log(forget), precision=HI)
        b_mid = b[CHUNK // 2:CHUNK // 2 + 1, :]
        b_last = b[CHUNK - 1:CHUNK, :]
        q_intra = q * jnp.exp(b - b_mid)
        k_intra = k * jnp.exp(b_mid - b)
        q_inter = q * jnp.exp(b)
        k_state = k * jnp.exp(b_last - b)
        decay = jnp.exp(b_last)
        cols = [slice(hh * HEAD_DIM, (hh + 1) * HEAD_DIM) for hh in range(heads)]
        vs = [proj_ref[rows, pl.ds(2 * d + hh * HEAD_DIM, HEAD_DIM)].astype(BF16) for hh in range(heads)]
        sts = [st_ref[hh] for hh in range(heads)]
        q_intra, k_intra = q_intra.astype(BF16), k_intra.astype(BF16)
        q_inter, k_state = q_inter.astype(BF16), k_state.astype(BF16)
        scores = [_dot_nt(q_intra[:, cs], k_intra[:, cs]) for cs in cols]
        inter = [_dot_nt(q_inter[:, cs], st.astype(BF16)) for cs, st in zip(cols, sts)]
        update = [_dot_tn(v, k_state[:, cs]) for cs, v in zip(cols, vs)]
        for hh in range(heads):
            cs = cols[hh]
            o = _dot(jnp.where(causal, scores[hh], 0.0).astype(BF16), vs[hh]) + inter[hh]
            st_ref[hh] = sts[hh] * decay[:, cs] + update[hh]
            gate = proj_ref[rows, pl.ds(3 * d + hh * HEAD_DIM, HEAD_DIM)]
            y = o * lax.rsqrt(jnp.mean(o * o, axis=-1, keepdims=True) + NORM_EPS)
            y_ref[rows, pl.ds(hh * HEAD_DIM, HEAD_DIM)] = (y * ngain * jax.nn.silu(gate)).astype(BF16)
        return carry

    lax.fori_loop(0, tt // CHUNK, chunk_body, 0, unroll=True)
    m = _dot(y_ref[...], wout_ref[...])
    o_ref[0] = x_ref[0] + _rms(m, gpost_ref[...])


def _hgrn2_call(x, gpre, gpost, w_in, lb_raw, ngain, w_out, *, layer, tt):
    bsz, t, d = x.shape
    heads = d // HEAD_DIM
    nl = lb_raw.shape[0]
    const = lambda b, i: (0, 0)
    return pl.pallas_call(
        functools.partial(_hgrn2_body, layer),
        grid=(bsz, t // tt),
        in_specs=[
            pl.BlockSpec((1, tt, d), lambda b, i: (b, i, 0)),
            pl.BlockSpec((1, d), const),
            pl.BlockSpec((1, d), const),
            pl.BlockSpec((d, 4 * d), const),
            pl.BlockSpec((nl, d), const),
            pl.BlockSpec((1, HEAD_DIM), const),
            pl.BlockSpec((d, d), const),
        ],
        out_specs=pl.BlockSpec((1, tt, d), lambda b, i: (b, i, 0)),
        out_shape=jax.ShapeDtypeStruct((bsz, t, d), F32),
        scratch_shapes=[
            pltpu.VMEM((tt, 4 * d), F32),
            pltpu.VMEM((tt, d), BF16),
            pltpu.VMEM((heads, HEAD_DIM, HEAD_DIM), F32),
        ],
        compiler_params=pltpu.CompilerParams(
            dimension_semantics=("parallel", "arbitrary"),
            vmem_limit_bytes=V7X_VMEM_LIMIT_BYTES),
        name="hgrn2_sublayer",
    )(x, gpre, gpost, w_in, lb_raw, ngain, w_out)


def _packed_matmul(xs, ys, left_half):
    out = []
    for x, y in zip(xs, ys):
        yb = y.astype(BF16)
        zero = jnp.zeros_like(yb)
        block_diag = jnp.concatenate([jnp.where(left_half, yb, zero), jnp.where(left_half, zero, yb)], axis=0)
        out.append(_dot(x.astype(BF16), block_diag))
    return out


def _unit_lower_inverse_minus_eye(a_list, diag_blocks, left_half):
    mm = functools.partial(_packed_matmul, left_half=left_half)
    dg = [jnp.where(diag_blocks, a, 0.0) for a in a_list]
    off = [a - d for a, d in zip(a_list, dg)]
    n = len(a_list)
    q = [-d for d in dg]
    p = mm(q, q)
    steps = INV_BLOCK.bit_length() - 2
    for step in range(steps):
        last = step == steps - 1
        prods = mm(p if last else p + p, q if last else q + p)
        q = [qi + pi + pqi for qi, pi, pqi in zip(q, p, prods[:n])]
        p = prods[n:]
    x = [-(o + qo) for o, qo in zip(off, mm(q, off))]
    x2 = mm(x, x)
    y = [xi + x2i + x3i for xi, x2i, x3i in zip(x, x2, mm(x, x2))]
    return [qi + yi + yqi for qi, yi, yqi in zip(q, y, mm(y, q))]


def _gdn_body(x_ref, gpre_ref, gpost_ref, wmain_ref, wba_ref, convw_ref, alog_ref, dtb_ref,
              ngain_ref, wout_ref, o_ref, ext_ref, qkv_ref, z_ref, y_ref, s_ref,
              gcol_ref, grow_ref, beta_ref, u_ref, lhs_a_ref, lhs_b_ref):
    tt, d = x_ref.shape[1], x_ref.shape[2]
    vheads = s_ref.shape[0]
    key = (ext_ref.shape[1] - vheads * HEAD_DIM) // 2
    qk_heads = key // HEAD_DIM
    assert vheads == 2 * qk_heads
    nconv = ext_ref.shape[1]
    nchunk = tt // CHUNK
    t_idx = pl.program_id(1)

    @pl.when(t_idx == 0)
    def _():
        s_ref[...] = jnp.zeros_like(s_ref)
        ext_ref[pl.ds(0, CONV_HIST), :] = jnp.zeros((CONV_HIST, nconv), F32)

    @pl.when(t_idx > 0)
    def _():
        ext_ref[pl.ds(0, CONV_HIST), :] = ext_ref[pl.ds(tt, CONV_HIST), :]

    h = _rms(x_ref[0], gpre_ref[...]).astype(BF16)
    pm = _dot(h, wmain_ref[...])
    ext_ref[pl.ds(CONV_HIST, tt), :] = pm[:, :nconv]
    z_ref[...] = pm[:, nconv:]
    pba = _dot(h, wba_ref[...])
    beta_ref[...] = jax.nn.sigmoid(pba[:, :vheads])
    g = -jnp.exp(alog_ref[...]) * jax.nn.softplus(pba[:, vheads:] + dtb_ref[...])

    acc = convw_ref[0:1, :] * ext_ref[pl.ds(CONV_HIST - CONV_K + 1, tt), :]
    for j in range(1, CONV_K):
        acc = acc + convw_ref[j:j + 1, :] * ext_ref[pl.ds(CONV_HIST - CONV_K + 1 + j, tt), :]
    qkv_ref[...] = jax.nn.silu(acc)
    for hq in range(2 * qk_heads):
        lanes = pl.ds(hq * HEAD_DIM, HEAD_DIM)
        a = qkv_ref[:, lanes]
        a = a * lax.rsqrt(jnp.sum(a * a, axis=-1, keepdims=True) + NORM_EPS)
        if hq < qk_heads:
            a = a * (HEAD_DIM ** -0.5)
        qkv_ref[:, lanes] = a

    tril = _tril_mask(CHUNK).astype(F32)
    r16 = lax.broadcasted_iota(jnp.int32, (vheads, vheads), 0)
    c16 = lax.broadcasted_iota(jnp.int32, (vheads, vheads), 1)
    eye_h = (r16 == c16).astype(F32)
    for c in range(nchunk):
        gc = g[c * CHUNK:(c + 1) * CHUNK, :]
        gcol_ref[pl.ds(c * CHUNK, CHUNK), :] = _dot(tril, gc, precision=HI)
        g_t = _dot_nt(eye_h, gc, precision=HI)
        grow_ref[c] = _dot_nt(g_t, tril, precision=HI)

    row = lax.broadcasted_iota(jnp.int32, (CHUNK, 2 * CHUNK), 0)
    lane = lax.broadcasted_iota(jnp.int32, (CHUNK, 2 * CHUNK), 1)
    col = lane % CHUNK
    left_half = lane < CHUNK
    incl = col <= row
    strict = col < row
    diag_blocks = (row // INV_BLOCK) == (col // INV_BLOCK)
    ngain = ngain_ref[...]

    def prepare_chunks(ci, carry):
        a_list, ctx = [], []
        for cc in range(PREP_CHUNKS):
            c = ci * PREP_CHUNKS + cc
            rows = pl.ds(pl.multiple_of(c * CHUNK, CHUNK), CHUNK)
            g_cols = gcol_ref[rows, :]
            g_rows = grow_ref[c]
            betas = beta_ref[rows, :]
            for hq in range(qk_heads):
                q = qkv_ref[rows, pl.ds(hq * HEAD_DIM, HEAD_DIM)]
                k = qkv_ref[rows, pl.ds(key + hq * HEAD_DIM, HEAD_DIM)]
                kb = k.astype(BF16)
                prods = _dot_nt(jnp.concatenate([kb, q.astype(BF16)], axis=0),
                                jnp.concatenate([kb, kb], axis=0))
                heads = (2 * hq, 2 * hq + 1)
                g_full = [jnp.broadcast_to(g_cols[:, hv:hv + 1], (CHUNK, HEAD_DIM)) for hv in heads]
                b_full = [jnp.broadcast_to(betas[:, hv:hv + 1], (CHUNK, HEAD_DIM)) for hv in heads]
                g_col = jnp.where(left_half, g_full[0], g_full[1])
                b_col = jnp.where(left_half, b_full[0], b_full[1])
                g_row = jnp.concatenate([g_rows[hv:hv + 1, :] for hv in heads], axis=1)
                gamma = jnp.exp(jnp.where(incl, g_col - g_row, -jnp.inf))
                a_list.append(jnp.where(strict, prods[:CHUNK] * gamma * b_col, 0.0))
                ctx.append((c, rows, hq, q, k, g_rows, g_full, b_full, prods[CHUNK:] * gamma))
        t_list = _unit_lower_inverse_minus_eye(a_list, diag_blocks, left_half)
        for t_m, (c, rows, hq, q, k, g_rows, g_full, b_full, qk_gamma) in zip(t_list, ctx):
            k_t = k.T
            for half in range(2):
                hv = 2 * hq + half
                half_lanes = slice(half * CHUNK, (half + 1) * CHUNK)
                v = qkv_ref[rows, pl.ds(2 * key + hv * HEAD_DIM, HEAD_DIM)]
                exp_g = jnp.exp(g_full[half])
                rhs = jnp.concatenate([v * b_full[half], k * (b_full[half] * exp_g)], axis=1)
                uw = rhs + _dot(t_m[:, half_lanes].astype(BF16), rhs.astype(BF16))
                u_ref[c, hv] = uw[:, :HEAD_DIM]
                lhs_a_ref[c, hv, pl.ds(0, CHUNK), :] = uw[:, HEAD_DIM:].astype(BF16)
                lhs_a_ref[c, hv, pl.ds(CHUNK, CHUNK), :] = (q * exp_g).astype(BF16)
                g_row = g_rows[hv:hv + 1, :]
                g_last = g_row[:, CHUNK - 1:CHUNK]
                lhs_b_ref[c, hv, pl.ds(0, CHUNK), :] = qk_gamma[:, half_lanes].astype(BF16)
                lhs_b_ref[c, hv, pl.ds(CHUNK, HEAD_DIM), :] = (k_t * jnp.exp(g_last - g_row)).astype(BF16)
        return carry

    def recur_chunk(c, carry):
        rows = pl.ds(pl.multiple_of(c * CHUNK, CHUNK), CHUNK)
        decays = jnp.exp(grow_ref[c][:, CHUNK - 1:CHUNK])
        states = [s_ref[hv] for hv in range(vheads)]
        ws_qs = [_dot(lhs_a_ref[c, hv], states[hv].astype(BF16)) for hv in range(vheads)]
        for hv in range(vheads):
            v_new = u_ref[c, hv] - ws_qs[hv][:CHUNK]
            upd = _dot(lhs_b_ref[c, hv], v_new.astype(BF16))
            o = ws_qs[hv][CHUNK:] + upd[:CHUNK]
            s_ref[hv] = decays[hv:hv + 1, :] * states[hv] + upd[CHUNK:]
            zg = z_ref[rows, pl.ds(hv * HEAD_DIM, HEAD_DIM)]
            y = o * lax.rsqrt(jnp.mean(o * o, axis=-1, keepdims=True) + NORM_EPS)
            y_ref[rows, pl.ds(hv * HEAD_DIM, HEAD_DIM)] = (y * ngain * jax.nn.silu(zg)).astype(BF16)
        return carry

    lax.fori_loop(0, nchunk // PREP_CHUNKS, prepare_chunks, 0)
    lax.fori_loop(0, nchunk, recur_chunk, 0, unroll=True)
    m = _dot(y_ref[...], wout_ref[...])
    o_ref[0] = x_ref[0] + _rms(m, gpost_ref[...])


def _gdn_call(x, gpre, gpost, w_main, w_ba, conv_w, a_log, dt_bias, ngain, w_out, *, tt):
    bsz, t, d = x.shape
    vheads = a_log.shape[1]
    nconv = conv_w.shape[1]
    val = vheads * HEAD_DIM
    nchunk = tt // CHUNK
    const = lambda b, i: (0, 0)
    resident = dict(pipeline_mode=pl.Buffered(1))
    return pl.pallas_call(
        _gdn_body,
        grid=(bsz, t // tt),
        in_specs=[
            pl.BlockSpec((1, tt, d), lambda b, i: (b, i, 0)),
            pl.BlockSpec((1, d), const),
            pl.BlockSpec((1, d), const),
            pl.BlockSpec((d, nconv + val), const, **resident),
            pl.BlockSpec((d, 2 * vheads), const),
            pl.BlockSpec((CONV_K, nconv), const),
            pl.BlockSpec((1, vheads), const),
            pl.BlockSpec((1, vheads), const),
            pl.BlockSpec((1, HEAD_DIM), const),
            pl.BlockSpec((val, d), const, **resident),
        ],
        out_specs=pl.BlockSpec((1, tt, d), lambda b, i: (b, i, 0)),
        out_shape=jax.ShapeDtypeStruct((bsz, t, d), F32),
        scratch_shapes=[
            pltpu.VMEM((tt + CONV_HIST, nconv), F32),
            pltpu.VMEM((tt, nconv), F32),
            pltpu.VMEM((tt, val), F32),
            pltpu.VMEM((tt, val), BF16),
            pltpu.VMEM((vheads, HEAD_DIM, HEAD_DIM), F32),
            pltpu.VMEM((tt, vheads), F32),
            pltpu.VMEM((nchunk, vheads, CHUNK), F32),
            pltpu.VMEM((tt, vheads), F32),
            pltpu.VMEM((nchunk, vheads, CHUNK, HEAD_DIM), F32),
            pltpu.VMEM((nchunk, vheads, 2 * CHUNK, HEAD_DIM), BF16),
            pltpu.VMEM((nchunk, vheads, CHUNK + HEAD_DIM, CHUNK), BF16),
        ],
        compiler_params=pltpu.CompilerParams(
            dimension_semantics=("parallel", "arbitrary"),
            vmem_limit_bytes=V7X_VMEM_LIMIT_BYTES),
        name="gdn_sublayer",
    )(x, gpre, gpost, w_main, w_ba, conv_w, a_log, dt_bias, ngain, w_out)


def _tile(n, pref):
    t = min(n, pref)
    assert n % t == 0, (n, t)
    return t


def kernel(x, norm_gains, ffn1_w_in, ffn1_w_out, ffn2_w_in, ffn2_w_out, hg_w_in, hg_lower_bounds,
           hg_norm_gain, hg_w_out, gd_w_in, gd_conv_w, gd_a_log, gd_dt_bias, gd_norm_gain, gd_w_out):
    bsz, t, d = x.shape
    depth = norm_gains.shape[0]
    n_mixers = 2
    dff = ffn1_w_out.shape[1]
    nconv = gd_conv_w.shape[2]
    vheads = gd_a_log.shape[1]
    n_main = nconv + vheads * HEAD_DIM

    tm = _tile(bsz * t, 1024)
    assert dff % FFN_COL_BLOCK == 0
    tt_hg = _tile(t, 512)
    tt_gd = _tile(t, 256)

    def row(v):
        return v.reshape(1, -1)

    def ffn(xc, gpre, gpost, w_in, w_out):
        y = _ffn_call(xc.reshape(bsz * t, d), row(gpre), row(gpost),
                      w_in.astype(BF16), w_out.astype(BF16), tm=tm, nsub=2)
        return y.reshape(bsz, t, d)

    for i in range(depth):
        ng = norm_gains[i]
        j = i // n_mixers
        x = ffn(x, ng[0], ng[1], ffn1_w_in[i], ffn1_w_out[i])
        if i % n_mixers == 0:
            x = _hgrn2_call(x, row(ng[2]), row(ng[3]), hg_w_in[j].astype(BF16), hg_lower_bounds,
                            row(hg_norm_gain[j]), hg_w_out[j].astype(BF16), layer=j, tt=tt_hg)
        else:
            w_in = gd_w_in[j].astype(BF16)
            x = _gdn_call(x, row(ng[2]), row(ng[3]), w_in[:, :n_main], w_in[:, n_main:],
                          gd_conv_w[j], row(gd_a_log[j]), row(gd_dt_bias[j]),
                          row(gd_norm_gain[j]), gd_w_out[j].astype(BF16), tt=tt_gd)
        x = ffn(x, ng[4], ng[5], ffn2_w_in[i], ffn2_w_out[i])
    return x
```

```python
import functools

import jax
import jax.numpy as jnp
from jax import lax
from jax.experimental import pallas as pl
from jax.experimental.pallas import tpu as pltpu

NORM_EPS = 1e-6
CHUNK = 64
HEAD_DIM = 128
CONV_K = 4
CONV_HIST = 8
INV_BLOCK = 16
FFN_COL_BLOCK = 256
PROJ_COL_GROUP = 1024
PREP_CHUNKS = 4
V7X_VMEM_LIMIT_BYTES = 58 * 1024 * 1024

F32 = jnp.float32
BF16 = jnp.bfloat16
HI = lax.Precision.HIGHEST


def _dot(a, b, precision=None):
    return jnp.dot(a, b, preferred_element_type=F32, precision=precision)


def _dot_nt(a, b, precision=None):
    return lax.dot_general(a, b, (((1,), (1,)), ((), ())),
                           preferred_element_type=F32, precision=precision)


def _dot_tn(a, b, precision=None):
    return lax.dot_general(a, b, (((0,), (0,)), ((), ())),
                           preferred_element_type=F32, precision=precision)


def _dot_zero_one_lhs(a, x):
    a = a.astype(BF16)
    hi = x.astype(BF16)
    rest = x - hi.astype(F32)
    mid = rest.astype(BF16)
    lo = (rest - mid.astype(F32)).astype(BF16)
    return _dot(a, hi) + _dot(a, mid) + _dot(a, lo)


def _chunk_rows(c):
    start = c * CHUNK
    return pl.ds(start if isinstance(start, int) else pl.multiple_of(start, CHUNK), CHUNK)


def _rms(x, gain):
    return x * lax.rsqrt(jnp.mean(x * x, axis=-1, keepdims=True) + NORM_EPS) * gain


def _tril_mask(n, k=0):
    r = lax.broadcasted_iota(jnp.int32, (n, n), 0)
    c = lax.broadcasted_iota(jnp.int32, (n, n), 1)
    return c <= r + k


def _ffn_body(x_ref, gpre_ref, gpost_ref, wi_ref, wo_ref, o_ref, act_ref):
    nsub, ts, dff = act_ref.shape
    d = x_ref.shape[1]
    nblk = dff // FFN_COL_BLOCK
    gpre, gpost = gpre_ref[...], gpost_ref[...]

    def pre_norm(s):
        return _rms(x_ref[pl.ds(s * ts, ts), :], gpre).astype(BF16)

    def hidden_block(s, h, jb):
        cols = slice(jb * FFN_COL_BLOCK, (jb + 1) * FFN_COL_BLOCK)
        up_cols = slice(dff + jb * FFN_COL_BLOCK, dff + (jb + 1) * FFN_COL_BLOCK)
        gate = _dot(h, wi_ref[:, cols])
        up = _dot(h, wi_ref[:, up_cols])
        act_ref[s, :, cols] = (jax.nn.silu(gate) * up).astype(BF16)

    def finish(s):
        rows = pl.ds(s * ts, ts)
        y = _dot(act_ref[s], wo_ref[...])
        o_ref[rows, :] = x_ref[rows, :] + 0.5 * _rms(y, gpost)

    h = pre_norm(0)
    for s in range(nsub):
        h_next = None
        for jb in range(nblk):
            hidden_block(s, h, jb)
            if jb == 0 and s + 1 < nsub:
                h_next = pre_norm(s + 1)
        if s > 0:
            finish(s - 1)
        h = h_next
    finish(nsub - 1)


def _ffn_call(x2, gpre, gpost, w_in, w_out, *, tm, nsub):
    n, d = x2.shape
    dff = w_out.shape[0]
    const = lambda i: (0, 0)
    resident = dict(pipeline_mode=pl.Buffered(1))
    return pl.pallas_call(
        _ffn_body,
        grid=(n // tm,),
        in_specs=[
            pl.BlockSpec((tm, d), lambda i: (i, 0)),
            pl.BlockSpec((1, d), const),
            pl.BlockSpec((1, d), const),
            pl.BlockSpec((d, 2 * dff), const, **resident),
            pl.BlockSpec((dff, d), const, **resident),
        ],
        out_specs=pl.BlockSpec((tm, d), lambda i: (i, 0)),
        out_shape=jax.ShapeDtypeStruct((n, d), F32),
        scratch_shapes=[pltpu.VMEM((nsub, tm // nsub, dff), BF16)],
        compiler_params=pltpu.CompilerParams(
            dimension_semantics=("parallel",),
            vmem_limit_bytes=V7X_VMEM_LIMIT_BYTES),
        name="ffn_sublayer",
    )(x2, gpre, gpost, w_in, w_out)


def _hgrn2_body(layer, x_ref, gpre_ref, gpost_ref, win_ref, lbraw_ref, ngain_ref, wout_ref,
                o_ref, proj_ref, y_ref, st_ref):
    tt, d = x_ref.shape[1], x_ref.shape[2]
    heads = d // HEAD_DIM

    @pl.when(pl.program_id(1) == 0)
    def _():
        st_ref[...] = jnp.zeros_like(st_ref)

    h = _rms(x_ref[0], gpre_ref[...]).astype(BF16)
    proj_ref[...] = _dot(h, win_ref[...])

    raw = lbraw_ref[...]
    e = jnp.exp(raw - jnp.max(raw, axis=0, keepdims=True))
    p = e / jnp.sum(e, axis=0, keepdims=True)
    first = p[0:1, :]
    cum = first
    for l in range(1, layer + 1):
        cum = cum + p[l:l + 1, :]
    lb = cum - first

    tril = _tril_mask(CHUNK).astype(F32)
    causal = _tril_mask(CHUNK)
    ngain = ngain_ref[...]

    def chunk_body(c, carry):
        rows = pl.ds(pl.multiple_of(c * CHUNK, CHUNK), CHUNK)
        q = jax.nn.silu(proj_ref[rows, pl.ds(0, d)])
        forget = lb + (1.0 - lb) * jax.nn.sigmoid(proj_ref[rows, pl.ds(d, d)])
        k = 1.0 - forget
        b = _dot_zero_one_lhs(tril, jnp.log(forget))
        b_mid = b[CHUNK // 2:CHUNK // 2 + 1, :]
        b_last = b[CHUNK - 1:CHUNK, :]
        q_intra = q * jnp.exp(b - b_mid)
        k_intra = k * jnp.exp(b_mid - b)
        q_inter = q * jnp.exp(b)
        k_state = k * jnp.exp(b_last - b)
        decay = jnp.exp(b_last)
        cols = [slice(hh * HEAD_DIM, (hh + 1) * HEAD_DIM) for hh in range(heads)]
        vs = [proj_ref[rows, pl.ds(2 * d + hh * HEAD_DIM, HEAD_DIM)].astype(BF16) for hh in range(heads)]
        sts = [st_ref[hh] for hh in range(heads)]
        q_intra, k_intra = q_intra.astype(BF16), k_intra.astype(BF16)
        q_inter, k_state = q_inter.astype(BF16), k_state.astype(BF16)
        scores = [_dot_nt(q_intra[:, cs], k_intra[:, cs]) for cs in cols]
        inter = [_dot_nt(q_inter[:, cs], st.astype(BF16)) for cs, st in zip(cols, sts)]
        update = [_dot_tn(v, k_state[:, cs]) for cs, v in zip(cols, vs)]
        for hh in range(heads):
            cs = cols[hh]
            o = _dot(jnp.where(causal, scores[hh], 0.0).astype(BF16), vs[hh]) + inter[hh]
            st_ref[hh] = sts[hh] * decay[:, cs] + update[hh]
            gate = proj_ref[rows, pl.ds(3 * d + hh * HEAD_DIM, HEAD_DIM)]
            y = o * lax.rsqrt(jnp.mean(o * o, axis=-1, keepdims=True) + NORM_EPS)
            y_ref[rows, pl.ds(hh * HEAD_DIM, HEAD_DIM)] = (y * ngain * jax.nn.silu(gate)).astype(BF16)
        return carry

    lax.fori_loop(0, tt // CHUNK, chunk_body, 0, unroll=True)
    m = _dot(y_ref[...], wout_ref[...])
    o_ref[0] = x_ref[0] + _rms(m, gpost_ref[...])


def _hgrn2_call(x, gpre, gpost, w_in, lb_raw, ngain, w_out, *, layer, tt):
    bsz, t, d = x.shape
    heads = d // HEAD_DIM
    nl = lb_raw.shape[0]
    const = lambda b, i: (0, 0)
    return pl.pallas_call(
        functools.partial(_hgrn2_body, layer),
        grid=(bsz, t // tt),
        in_specs=[
            pl.BlockSpec((1, tt, d), lambda b, i: (b, i, 0)),
            pl.BlockSpec((1, d), const),
            pl.BlockSpec((1, d), const),
            pl.BlockSpec((d, 4 * d), const),
            pl.BlockSpec((nl, d), const),
            pl.BlockSpec((1, HEAD_DIM), const),
            pl.BlockSpec((d, d), const),
        ],
        out_specs=pl.BlockSpec((1, tt, d), lambda b, i: (b, i, 0)),
        out_shape=jax.ShapeDtypeStruct((bsz, t, d), F32),
        scratch_shapes=[
            pltpu.VMEM((tt, 4 * d), F32),
            pltpu.VMEM((tt, d), BF16),
            pltpu.VMEM((heads, HEAD_DIM, HEAD_DIM), F32),
        ],
        compiler_params=pltpu.CompilerParams(
            dimension_semantics=("parallel", "arbitrary"),
            vmem_limit_bytes=V7X_VMEM_LIMIT_BYTES),
        name="hgrn2_sublayer",
    )(x, gpre, gpost, w_in, lb_raw, ngain, w_out)


def _packed_matmul(xs, ys, left_half):
    out = []
    for x, y in zip(xs, ys):
        yb = y.astype(BF16)
        zero = jnp.zeros_like(yb)
        block_diag = jnp.concatenate([jnp.where(left_half, yb, zero), jnp.where(left_half, zero, yb)], axis=0)
        out.append(_dot(x.astype(BF16), block_diag))
    return out


def _unit_lower_inverse_minus_eye(a_list, diag_blocks, left_half):
    mm = functools.partial(_packed_matmul, left_half=left_half)

    dg = [jnp.where(diag_blocks, a, 0.0) for a in a_list]
    off = [a - d for a, d in zip(a_list, dg)]
    n = len(a_list)
    q = [-d for d in dg]
    p = mm(q, q)
    steps = INV_BLOCK.bit_length() - 2
    for step in range(steps):
        last = step == steps - 1
        prods = mm(p if last else p + p, q if last else q + p)
        q = [qi + pi + pqi for qi, pi, pqi in zip(q, p, prods[:n])]
        p = prods[n:]
    x = [-(o + qo) for o, qo in zip(off, mm(q, off))]
    x2 = mm(x, x)
    y = [xi + x2i + x3i for xi, x2i, x3i in zip(x, x2, mm(x, x2))]
    return [qi + yi + yqi for qi, yi, yqi in zip(q, y, mm(y, q))]


def _gdn_body(x_ref, gpre_ref, gpost_ref, wmain_ref, wba_ref, convw_ref, alog_ref, dtb_ref,
              ngain_ref, wout_ref, o_ref, ext_ref, hist_ref, qkv_ref, z_ref, y_ref, s_ref,
              gcol_ref, grow_ref, beta_ref, u_ref, lhs_a_ref, lhs_b_ref):
    tt, d = x_ref.shape[1], x_ref.shape[2]
    vheads = beta_ref.shape[1]
    nconv = qkv_ref.shape[1]
    key = (nconv - vheads * HEAD_DIM) // 2
    qk_heads = key // HEAD_DIM
    assert vheads == 2 * qk_heads
    assert key % PROJ_COL_GROUP == 0 and z_ref.shape[1] % PROJ_COL_GROUP == 0
    nchunk = tt // CHUNK
    t_idx = pl.program_id(1)

    @pl.when(t_idx == 0)
    def _():
        s_ref[...] = jnp.zeros_like(s_ref)
        hist_ref[...] = jnp.zeros_like(hist_ref)

    h = _rms(x_ref[0], gpre_ref[...]).astype(BF16)
    pba = _dot(h, wba_ref[...])
    beta_ref[...] = jax.nn.sigmoid(pba[:, :vheads])
    g = -jnp.exp(alog_ref[...]) * jax.nn.softplus(pba[:, vheads:] + dtb_ref[...])

    def project(c0):
        cols = pl.ds(c0, PROJ_COL_GROUP)
        pm = _dot(h, wmain_ref[:, cols])
        if c0 >= nconv:
            z_ref[:, pl.ds(c0 - nconv, PROJ_COL_GROUP)] = pm
            return
        ext_ref[pl.ds(0, CONV_HIST), :] = hist_ref[:, cols]
        ext_ref[pl.ds(CONV_HIST, tt), :] = pm
        hist_ref[:, cols] = pm[tt - CONV_HIST:, :]
        acc = convw_ref[0:1, cols] * ext_ref[pl.ds(CONV_HIST - CONV_K + 1, tt), :]
        for j in range(1, CONV_K):
            acc = acc + convw_ref[j:j + 1, cols] * ext_ref[pl.ds(CONV_HIST - CONV_K + 1 + j, tt), :]
        act = jax.nn.silu(acc)
        if c0 >= 2 * key:
            qkv_ref[:, cols] = act
            return
        for hh in range(PROJ_COL_GROUP // HEAD_DIM):
            a = act[:, hh * HEAD_DIM:(hh + 1) * HEAD_DIM]
            a = a * lax.rsqrt(jnp.sum(a * a, axis=-1, keepdims=True) + NORM_EPS)
            if c0 < key:
                a = a * (HEAD_DIM ** -0.5)
            qkv_ref[:, pl.ds(c0 + hh * HEAD_DIM, HEAD_DIM)] = a

    for c0 in range(0, nconv + z_ref.shape[1], PROJ_COL_GROUP):
        project(c0)

    tril = _tril_mask(CHUNK).astype(F32)
    r16 = lax.broadcasted_iota(jnp.int32, (vheads, vheads), 0)
    c16 = lax.broadcasted_iota(jnp.int32, (vheads, vheads), 1)
    eye_h = (r16 == c16).astype(F32)
    for c in range(nchunk):
        gc = g[c * CHUNK:(c + 1) * CHUNK, :]
        gcol_ref[pl.ds(c * CHUNK, CHUNK), :] = _dot(tril, gc, precision=HI)
        g_t = _dot_nt(eye_h, gc, precision=HI)
        grow_ref[c] = _dot_nt(g_t, tril, precision=HI)

    row = lax.broadcasted_iota(jnp.int32, (CHUNK, 2 * CHUNK), 0)
    lane = lax.broadcasted_iota(jnp.int32, (CHUNK, 2 * CHUNK), 1)
    col = lane % CHUNK
    left_half = lane < CHUNK
    incl = col <= row
    strict = col < row
    diag_blocks = (row // INV_BLOCK) == (col // INV_BLOCK)
    left_row = lax.broadcasted_iota(jnp.int32, (1, 2 * CHUNK), 1) < CHUNK
    ngain = ngain_ref[...]

    def prepare_chunks(ci):
        a_list, ctx = [], []
        for cc in range(PREP_CHUNKS):
            c = ci * PREP_CHUNKS + cc
            rows = _chunk_rows(c)
            g_cols = gcol_ref[rows, :]
            g_rows = grow_ref[c]
            betas = beta_ref[rows, :]
            for hq in range(qk_heads):
                q = qkv_ref[rows, pl.ds(hq * HEAD_DIM, HEAD_DIM)]
                k = qkv_ref[rows, pl.ds(key + hq * HEAD_DIM, HEAD_DIM)]
                kb = k.astype(BF16)
                prods = _dot_nt(jnp.concatenate([kb, q.astype(BF16)], axis=0),
                                jnp.concatenate([kb, kb], axis=0))
                heads = (2 * hq, 2 * hq + 1)
                g_full = [jnp.broadcast_to(g_cols[:, hv:hv + 1], (CHUNK, HEAD_DIM)) for hv in heads]
                b_full = [jnp.broadcast_to(betas[:, hv:hv + 1], (CHUNK, HEAD_DIM)) for hv in heads]
                g_col = jnp.where(left_half, g_full[0], g_full[1])
                b_col = jnp.where(left_half, b_full[0], b_full[1])
                g_row = jnp.concatenate([g_rows[hv:hv + 1, :] for hv in heads], axis=1)
                gamma = jnp.exp(jnp.where(incl, g_col - g_row, -jnp.inf))
                a_list.append(jnp.where(strict, prods[:CHUNK] * gamma * b_col, 0.0))
                ctx.append((cc, rows, hq, q, k, g_rows, g_row, g_full, b_full, prods[CHUNK:] * gamma))
        t_list = _unit_lower_inverse_minus_eye(a_list, diag_blocks, left_half)
        for t_m, (cc, rows, hq, q, k, g_rows, g_row, g_full, b_full, qk_gamma) in zip(t_list, ctx):
            heads = (2 * hq, 2 * hq + 1)
            k_t = k.T
            exp_g = [jnp.exp(gf) for gf in g_full]
            rhs = [jnp.concatenate([qkv_ref[rows, pl.ds(2 * key + hv * HEAD_DIM, HEAD_DIM)] * bf,
                                    k * (bf * eg)], axis=1)
                   for hv, bf, eg in zip(heads, b_full, exp_g)]
            rhs_both = jnp.concatenate(rhs, axis=0).astype(BF16)
            t_b = t_m.astype(BF16)
            zero = jnp.zeros_like(t_b)
            uw = [rhs[0] + _dot(jnp.where(left_half, t_b, zero), rhs_both),
                  rhs[1] + _dot(jnp.where(left_half, zero, t_b), rhs_both)]
            u_ref[cc, hq] = jnp.concatenate([uw[0][:, :HEAD_DIM], uw[1][:, :HEAD_DIM]], axis=1)
            lhs_a_ref[cc, hq, pl.ds(0, CHUNK), :] = jnp.concatenate(
                [uw[0][:, HEAD_DIM:], uw[1][:, HEAD_DIM:]], axis=1).astype(BF16)
            lhs_a_ref[cc, hq, pl.ds(CHUNK, CHUNK), :] = jnp.concatenate(
                [q * exp_g[0], q * exp_g[1]], axis=1).astype(BF16)
            g_last = jnp.where(left_row, g_rows[heads[0]:heads[0] + 1, CHUNK - 1:CHUNK],
                               g_rows[heads[1]:heads[1] + 1, CHUNK - 1:CHUNK])
            lhs_b_ref[cc, hq, pl.ds(0, CHUNK), :] = qk_gamma.astype(BF16)
            lhs_b_ref[cc, hq, pl.ds(CHUNK, HEAD_DIM), :] = (
                jnp.concatenate([k_t, k_t], axis=1) * jnp.exp(g_last - g_row)).astype(BF16)

    left_head = lax.broadcasted_iota(jnp.int32, (1, 2 * HEAD_DIM), 1) < HEAD_DIM

    def block_diag(pair):
        zero = jnp.zeros_like(pair)
        return jnp.concatenate([jnp.where(left_head, pair, zero), jnp.where(left_head, zero, pair)], axis=0)

    def recur_chunk(c, cc):
        rows = _chunk_rows(c)
        decays = jnp.exp(grow_ref[c][:, CHUNK - 1:CHUNK])
        states = [s_ref[hq] for hq in range(qk_heads)]
        ws_qs = [_dot(lhs_a_ref[cc, hq], block_diag(states[hq].astype(BF16))) for hq in range(qk_heads)]
        for hq in range(qk_heads):
            v_new = u_ref[cc, hq] - ws_qs[hq][:CHUNK]
            upd = _dot(lhs_b_ref[cc, hq], block_diag(v_new.astype(BF16)))
            o_pair = ws_qs[hq][CHUNK:] + upd[:CHUNK]
            decay = jnp.where(left_head, decays[2 * hq:2 * hq + 1, :], decays[2 * hq + 1:2 * hq + 2, :])
            s_ref[hq] = decay * states[hq] + upd[CHUNK:]
            for half in range(2):
                lanes = pl.ds((2 * hq + half) * HEAD_DIM, HEAD_DIM)
                o = o_pair[:, half * HEAD_DIM:(half + 1) * HEAD_DIM]
                y = o * lax.rsqrt(jnp.mean(o * o, axis=-1, keepdims=True) + NORM_EPS)
                y_ref[rows, lanes] = (y * ngain * jax.nn.silu(z_ref[rows, lanes])).astype(BF16)

    def chunk_group(gi, carry):
        prepare_chunks(gi)
        for cc in range(PREP_CHUNKS):
            recur_chunk(gi * PREP_CHUNKS + cc, cc)
        return carry

    if nchunk == PREP_CHUNKS:
        chunk_group(0, 0)
    else:
        lax.fori_loop(0, nchunk // PREP_CHUNKS, chunk_group, 0)
    m = _dot(y_ref[...], wout_ref[...])
    o_ref[0] = x_ref[0] + _rms(m, gpost_ref[...])


def _gdn_call(x, gpre, gpost, w_main, w_ba, conv_w, a_log, dt_bias, ngain, w_out, *, tt):
    bsz, t, d = x.shape
    vheads = a_log.shape[1]
    nconv = conv_w.shape[1]
    val = vheads * HEAD_DIM
    nchunk = tt // CHUNK
    const = lambda b, i: (0, 0)
    resident = dict(pipeline_mode=pl.Buffered(1))
    return pl.pallas_call(
        _gdn_body,
        grid=(bsz, t // tt),
        in_specs=[
            pl.BlockSpec((1, tt, d), lambda b, i: (b, i, 0)),
            pl.BlockSpec((1, d), const),
            pl.BlockSpec((1, d), const),
            pl.BlockSpec((d, nconv + val), const, **resident),
            pl.BlockSpec((d, 2 * vheads), const),
            pl.BlockSpec((CONV_K, nconv), const),
            pl.BlockSpec((1, vheads), const),
            pl.BlockSpec((1, vheads), const),
            pl.BlockSpec((1, HEAD_DIM), const),
            pl.BlockSpec((val, d), const, **resident),
        ],
        out_specs=pl.BlockSpec((1, tt, d), lambda b, i: (b, i, 0)),
        out_shape=jax.ShapeDtypeStruct((bsz, t, d), F32),
        scratch_shapes=[
            pltpu.VMEM((tt + CONV_HIST, PROJ_COL_GROUP), F32),
            pltpu.VMEM((CONV_HIST, nconv), F32),
            pltpu.VMEM((tt, nconv), F32),
            pltpu.VMEM((tt, val), F32),
            pltpu.VMEM((tt, val), BF16),
            pltpu.VMEM((vheads // 2, HEAD_DIM, 2 * HEAD_DIM), F32),
            pltpu.VMEM((tt, vheads), F32),
            pltpu.VMEM((nchunk, vheads, CHUNK), F32),
            pltpu.VMEM((tt, vheads), F32),
            pltpu.VMEM((PREP_CHUNKS, vheads // 2, CHUNK, 2 * HEAD_DIM), F32),
            pltpu.VMEM((PREP_CHUNKS, vheads // 2, 2 * CHUNK, 2 * HEAD_DIM), BF16),
            pltpu.VMEM((PREP_CHUNKS, vheads // 2, CHUNK + HEAD_DIM, 2 * CHUNK), BF16),
        ],
        compiler_params=pltpu.CompilerParams(
            dimension_semantics=("parallel", "arbitrary"),
            vmem_limit_bytes=V7X_VMEM_LIMIT_BYTES),
        name="gdn_sublayer",
    )(x, gpre, gpost, w_main, w_ba, conv_w, a_log, dt_bias, ngain, w_out)


def _tile(n, pref):
    t = min(n, pref)
    assert n % t == 0, (n, t)
    return t


def kernel(x, norm_gains, ffn1_w_in, ffn1_w_out, ffn2_w_in, ffn2_w_out, hg_w_in, hg_lower_bounds,
           hg_norm_gain, hg_w_out, gd_w_in, gd_conv_w, gd_a_log, gd_dt_bias, gd_norm_gain, gd_w_out):
    bsz, t, d = x.shape
    depth = norm_gains.shape[0]
    n_mixers = 2
    dff = ffn1_w_out.shape[1]
    nconv = gd_conv_w.shape[2]
    vheads = gd_a_log.shape[1]
    n_main = nconv + vheads * HEAD_DIM

    tm = _tile(bsz * t, 1024)
    assert dff % FFN_COL_BLOCK == 0
    tt_hg = _tile(t, 512)
    tt_gd = _tile(t, 256)

    def row(v):
        return v.reshape(1, -1)

    def ffn(xc, gpre, gpost, w_in, w_out):
        y = _ffn_call(xc.reshape(bsz * t, d), row(gpre), row(gpost),
                      w_in.astype(BF16), w_out.astype(BF16), tm=tm, nsub=2)
        return y.reshape(bsz, t, d)

    for i in range(depth):
        ng = norm_gains[i]
        j = i // n_mixers
        x = ffn(x, ng[0], ng[1], ffn1_w_in[i], ffn1_w_out[i])
        if i % n_mixers == 0:
            x = _hgrn2_call(x, row(ng[2]), row(ng[3]), hg_w_in[j].astype(BF16), hg_lower_bounds,
                            row(hg_norm_gain[j]), hg_w_out[j].astype(BF16), layer=j, tt=tt_hg)
        else:
            w_in = gd_w_in[j].astype(BF16)
            x = _gdn_call(x, row(ng[2]), row(ng[3]), w_in[:, :n_main], w_in[:, n_main:],
                          gd_conv_w[j], row(gd_a_log[j]), row(gd_dt_bias[j]),
                          row(gd_norm_gain[j]), gd_w_out[j].astype(BF16), tt=tt_gd)
        x = ffn(x, ng[4], ng[5], ffn2_w_in[i], ffn2_w_out[i])
    return x
```

```python
import functools

import jax
import jax.numpy as jnp
from jax import lax
from jax.experimental import pallas as pl
from jax.experimental.pallas import tpu as pltpu

NORM_EPS = 1e-6
CHUNK = 64
HEAD_DIM = 128
CONV_K = 4
CONV_HIST = 8
INV_BLOCK = 16
FFN_COL_BLOCK = 256
PROJ_COL_GROUP = 1024
PREP_CHUNKS = 4
V7X_VMEM_LIMIT_BYTES = 58 * 1024 * 1024

F32 = jnp.float32
BF16 = jnp.bfloat16
HI = lax.Precision.HIGHEST


def _dot(a, b, precision=None):
    return jnp.dot(a, b, preferred_element_type=F32, precision=precision)


def _dot_nt(a, b, precision=None):
    return lax.dot_general(a, b, (((1,), (1,)), ((), ())),
                           preferred_element_type=F32, precision=precision)


def _dot_tn(a, b, precision=None):
    return lax.dot_general(a, b, (((0,), (0,)), ((), ())),
                           preferred_element_type=F32, precision=precision)


def _dot_zero_one_lhs(a, x):
    a = a.astype(BF16)
    hi = x.astype(BF16)
    rest = x - hi.astype(F32)
    mid = rest.astype(BF16)
    lo = (rest - mid.astype(F32)).astype(BF16)
    return _dot(a, hi) + _dot(a, mid) + _dot(a, lo)


def _chunk_rows(c):
    start = c * CHUNK
    return pl.ds(start if isinstance(start, int) else pl.multiple_of(start, CHUNK), CHUNK)


def _rms(x, gain):
    return x * lax.rsqrt(jnp.mean(x * x, axis=-1, keepdims=True) + NORM_EPS) * gain


def _tril_mask(n, k=0):
    r = lax.broadcasted_iota(jnp.int32, (n, n), 0)
    c = lax.broadcasted_iota(jnp.int32, (n, n), 1)
    return c <= r + k


def _ffn_body(x_ref, xnext_ref, gpre_ref, gpost_ref, wi_ref, wo_ref, o_ref, act_ref, h0_ref):
    nsub, ts, dff = act_ref.shape
    nblk = dff // FFN_COL_BLOCK
    gpre = gpre_ref[...]
    half_gpost = 0.5 * gpost_ref[...]

    def pre_norm(s):
        return _rms(x_ref[pl.ds(s * ts, ts), :], gpre).astype(BF16)

    @pl.when(pl.program_id(0) == 0)
    def _():
        h0_ref[...] = pre_norm(0)

    def hidden_block(s, h, jb):
        cols = slice(jb * FFN_COL_BLOCK, (jb + 1) * FFN_COL_BLOCK)
        up_cols = slice(dff + jb * FFN_COL_BLOCK, dff + (jb + 1) * FFN_COL_BLOCK)
        gate = _dot(h, wi_ref[:, cols])
        up = _dot(h, wi_ref[:, up_cols])
        act_ref[s, :, cols] = (jax.nn.silu(gate) * up).astype(BF16)

    def finish(s):
        rows = pl.ds(s * ts, ts)
        y = _dot(act_ref[s], wo_ref[...])
        o_ref[rows, :] = x_ref[rows, :] + _rms(y, half_gpost)

    h = h0_ref[...]
    for s in range(nsub):
        h_next = None
        for jb in range(nblk):
            hidden_block(s, h, jb)
            if jb == 0 and s + 1 < nsub:
                h_next = pre_norm(s + 1)
            if jb == 0 and s + 1 == nsub:
                h0_ref[...] = _rms(xnext_ref[...], gpre).astype(BF16)
        if s > 0:
            finish(s - 1)
        h = h_next
    finish(nsub - 1)


def _ffn_call(x2, gpre, gpost, w_in, w_out, *, tm, nsub):
    n, d = x2.shape
    dff = w_out.shape[0]
    ts = tm // nsub
    const = lambda i: (0, 0)
    resident = dict(pipeline_mode=pl.Buffered(1))
    return pl.pallas_call(
        _ffn_body,
        grid=(n // tm,),
        in_specs=[
            pl.BlockSpec((tm, d), lambda i: (i, 0)),
            pl.BlockSpec((ts, d), lambda i: (jnp.minimum((i + 1) * nsub, n // ts - 1), 0)),
            pl.BlockSpec((1, d), const),
            pl.BlockSpec((1, d), const),
            pl.BlockSpec((d, 2 * dff), const, **resident),
            pl.BlockSpec((dff, d), const, **resident),
        ],
        out_specs=pl.BlockSpec((tm, d), lambda i: (i, 0)),
        out_shape=jax.ShapeDtypeStruct((n, d), F32),
        scratch_shapes=[pltpu.VMEM((nsub, ts, dff), BF16), pltpu.VMEM((ts, d), BF16)],
        compiler_params=pltpu.CompilerParams(
            dimension_semantics=("arbitrary",),
            vmem_limit_bytes=V7X_VMEM_LIMIT_BYTES),
        name="ffn_sublayer",
    )(x2, x2, gpre, gpost, w_in, w_out)


def _hgrn2_body(layer, x_ref, gpre_ref, gpost_ref, win_ref, lbraw_ref, ngain_ref, wout_ref,
                o_ref, proj_ref, y_ref, st_ref):
    tt, d = x_ref.shape[1], x_ref.shape[2]
    heads = d // HEAD_DIM

    @pl.when(pl.program_id(1) == 0)
    def _():
        st_ref[...] = jnp.zeros_like(st_ref)

    h = _rms(x_ref[0], gpre_ref[...]).astype(BF16)
    proj_ref[...] = _dot(h, win_ref[...])

    raw = lbraw_ref[...]
    e = jnp.exp(raw - jnp.max(raw, axis=0, keepdims=True))
    p = e / jnp.sum(e, axis=0, keepdims=True)
    first = p[0:1, :]
    cum = first
    for l in range(1, layer + 1):
        cum = cum + p[l:l + 1, :]
    lb = cum - first

    tril = _tril_mask(CHUNK).astype(F32)
    causal = _tril_mask(CHUNK)
    ngain = ngain_ref[...]

    def chunk_body(c, carry):
        rows = pl.ds(pl.multiple_of(c * CHUNK, CHUNK), CHUNK)
        q = jax.nn.silu(proj_ref[rows, pl.ds(0, d)])
        forget = lb + (1.0 - lb) * jax.nn.sigmoid(proj_ref[rows, pl.ds(d, d)])
        k = 1.0 - forget
        b = _dot_zero_one_lhs(tril, jnp.log(forget))
        b_mid = b[CHUNK // 2:CHUNK // 2 + 1, :]
        b_last = b[CHUNK - 1:CHUNK, :]
        q_intra = q * jnp.exp(b - b_mid)
        k_intra = k * jnp.exp(b_mid - b)
        q_inter = q * jnp.exp(b)
        k_state = k * jnp.exp(b_last - b)
        decay = jnp.exp(b_last)
        cols = [slice(hh * HEAD_DIM, (hh + 1) * HEAD_DIM) for hh in range(heads)]
        vs = [proj_ref[rows, pl.ds(2 * d + hh * HEAD_DIM, HEAD_DIM)].astype(BF16) for hh in range(heads)]
        sts = [st_ref[hh] for hh in range(heads)]
        q_intra, k_intra = q_intra.astype(BF16), k_intra.astype(BF16)
        q_inter, k_state = q_inter.astype(BF16), k_state.astype(BF16)
        scores = [_dot_nt(q_intra[:, cs], k_intra[:, cs]) for cs in cols]
        inter = [_dot_nt(q_inter[:, cs], st.astype(BF16)) for cs, st in zip(cols, sts)]
        update = [_dot_tn(v, k_state[:, cs]) for cs, v in zip(cols, vs)]
        for hh in range(heads):
            cs = cols[hh]
            o = _dot(jnp.where(causal, scores[hh], 0.0).astype(BF16), vs[hh]) + inter[hh]
            st_ref[hh] = sts[hh] * decay[:, cs] + update[hh]
            gate = proj_ref[rows, pl.ds(3 * d + hh * HEAD_DIM, HEAD_DIM)]
            y = o * lax.rsqrt(jnp.mean(o * o, axis=-1, keepdims=True) + NORM_EPS)
            y_ref[rows, pl.ds(hh * HEAD_DIM, HEAD_DIM)] = (y * ngain * jax.nn.silu(gate)).astype(BF16)
        return carry

    lax.fori_loop(0, tt // CHUNK, chunk_body, 0, unroll=True)
    m = _dot(y_ref[...], wout_ref[...])
    o_ref[0] = x_ref[0] + _rms(m, gpost_ref[...])


def _hgrn2_call(x, gpre, gpost, w_in, lb_raw, ngain, w_out, *, layer, tt):
    bsz, t, d = x.shape
    heads = d // HEAD_DIM
    nl = lb_raw.shape[0]
    const = lambda b, i: (0, 0)
    return pl.pallas_call(
        functools.partial(_hgrn2_body, layer),
        grid=(bsz, t // tt),
        in_specs=[
            pl.BlockSpec((1, tt, d), lambda b, i: (b, i, 0)),
            pl.BlockSpec((1, d), const),
            pl.BlockSpec((1, d), const),
            pl.BlockSpec((d, 4 * d), const),
            pl.BlockSpec((nl, d), const),
            pl.BlockSpec((1, HEAD_DIM), const),
            pl.BlockSpec((d, d), const),
        ],
        out_specs=pl.BlockSpec((1, tt, d), lambda b, i: (b, i, 0)),
        out_shape=jax.ShapeDtypeStruct((bsz, t, d), F32),
        scratch_shapes=[
            pltpu.VMEM((tt, 4 * d), F32),
            pltpu.VMEM((tt, d), BF16),
            pltpu.VMEM((heads, HEAD_DIM, HEAD_DIM), F32),
        ],
        compiler_params=pltpu.CompilerParams(
            dimension_semantics=("parallel", "arbitrary"),
            vmem_limit_bytes=V7X_VMEM_LIMIT_BYTES),
        name="hgrn2_sublayer",
    )(x, gpre, gpost, w_in, lb_raw, ngain, w_out)


def _packed_matmul(xs, ys, left_half):
    out = []
    for x, y in zip(xs, ys):
        yb = y.astype(BF16)
        zero = jnp.zeros_like(yb)
        block_diag = jnp.concatenate([jnp.where(left_half, yb, zero), jnp.where(left_half, zero, yb)], axis=0)
        out.append(_dot(x.astype(BF16), block_diag))
    return out


def _unit_lower_inverse_minus_eye(a_list, diag_blocks, left_half):
    mm = functools.partial(_packed_matmul, left_half=left_half)

    dg = [jnp.where(diag_blocks, a, 0.0) for a in a_list]
    off = [a - d for a, d in zip(a_list, dg)]
    n = len(a_list)
    q = [-d for d in dg]
    p = mm(q, q)
    steps = INV_BLOCK.bit_length() - 2
    for step in range(steps):
        last = step == steps - 1
        prods = mm(p if last else p + p, q if last else q + p)
        q = [qi + pi + pqi for qi, pi, pqi in zip(q, p, prods[:n])]
        p = prods[n:]
    x = [-(o + qo) for o, qo in zip(off, mm(q, off))]
    x2 = mm(x, x)
    y = [xi + x2i + x3i for xi, x2i, x3i in zip(x, x2, mm(x, x2))]
    return [qi + yi + yqi for qi, yi, yqi in zip(q, y, mm(y, q))]


def _gdn_body(x_ref, gpre_ref, gpost_ref, wmain_ref, wba_ref, convw_ref, alog_ref, dtb_ref,
              ngain_ref, wout_ref, o_ref, ext_ref, hist_ref, qkv_ref, z_ref, y_ref, s_ref,
              gcol_ref, grow_ref, beta_ref, u_ref, lhs_a_ref, lhs_b_ref):
    tt, d = x_ref.shape[1], x_ref.shape[2]
    vheads = beta_ref.shape[1]
    nconv = qkv_ref.shape[1]
    key = (nconv - vheads * HEAD_DIM) // 2
    qk_heads = key // HEAD_DIM
    assert vheads == 2 * qk_heads
    assert key % PROJ_COL_GROUP == 0 and z_ref.shape[1] % PROJ_COL_GROUP == 0
    nchunk = tt // CHUNK
    t_idx = pl.program_id(1)

    @pl.when(t_idx == 0)
    def _():
        s_ref[...] = jnp.zeros_like(s_ref)
        hist_ref[...] = jnp.zeros_like(hist_ref)

    h = _rms(x_ref[0], gpre_ref[...]).astype(BF16)
    pba = _dot(h, wba_ref[...])
    beta_ref[...] = jax.nn.sigmoid(pba[:, :vheads])
    g = -jnp.exp(alog_ref[...]) * jax.nn.softplus(pba[:, vheads:] + dtb_ref[...])

    def project(c0):
        cols = pl.ds(c0, PROJ_COL_GROUP)
        pm = _dot(h, wmain_ref[:, cols])
        if c0 >= nconv:
            z_ref[:, pl.ds(c0 - nconv, PROJ_COL_GROUP)] = pm
            return
        ext_ref[pl.ds(0, CONV_HIST), :] = hist_ref[:, cols]
        ext_ref[pl.ds(CONV_HIST, tt), :] = pm
        hist_ref[:, cols] = pm[tt - CONV_HIST:, :]
        xe = ext_ref[...]
        acc = convw_ref[0:1, cols] * xe
        for j in range(1, CONV_K):
            acc = convw_ref[j:j + 1, cols] * xe + pltpu.roll(acc, shift=1, axis=0)
        act = jax.nn.silu(acc[CONV_HIST:, :])
        if c0 >= 2 * key:
            qkv_ref[:, cols] = act
            return
        for hh in range(PROJ_COL_GROUP // HEAD_DIM):
            a = act[:, hh * HEAD_DIM:(hh + 1) * HEAD_DIM]
            a = a * lax.rsqrt(jnp.sum(a * a, axis=-1, keepdims=True) + NORM_EPS)
            if c0 < key:
                a = a * (HEAD_DIM ** -0.5)
            qkv_ref[:, pl.ds(c0 + hh * HEAD_DIM, HEAD_DIM)] = a

    for c0 in range(0, nconv + z_ref.shape[1], PROJ_COL_GROUP):
        project(c0)

    tril = _tril_mask(CHUNK).astype(F32)
    r16 = lax.broadcasted_iota(jnp.int32, (vheads, vheads), 0)
    c16 = lax.broadcasted_iota(jnp.int32, (vheads, vheads), 1)
    eye_h = (r16 == c16).astype(F32)
    for c in range(nchunk):
        gc = g[c * CHUNK:(c + 1) * CHUNK, :]
        gcol_ref[pl.ds(c * CHUNK, CHUNK), :] = _dot(tril, gc, precision=HI)
        g_t = _dot_nt(eye_h, gc, precision=HI)
        grow_ref[c] = _dot_nt(g_t, tril, precision=HI)

    row = lax.broadcasted_iota(jnp.int32, (CHUNK, 2 * CHUNK), 0)
    lane = lax.broadcasted_iota(jnp.int32, (CHUNK, 2 * CHUNK), 1)
    col = lane % CHUNK
    left_half = lane < CHUNK
    incl = col <= row
    strict = col < row
    diag_blocks = (row // INV_BLOCK) == (col // INV_BLOCK)
    left_row = lax.broadcasted_iota(jnp.int32, (1, 2 * CHUNK), 1) < CHUNK
    ngain = ngain_ref[...]

    def prepare_chunks(ci):
        a_list, ctx = [], []
        for cc in range(PREP_CHUNKS):
            c = ci * PREP_CHUNKS + cc
            rows = _chunk_rows(c)
            g_cols = gcol_ref[rows, :]
            g_rows = grow_ref[c]
            betas = beta_ref[rows, :]
            for hq in range(qk_heads):
                q = qkv_ref[rows, pl.ds(hq * HEAD_DIM, HEAD_DIM)]
                k = qkv_ref[rows, pl.ds(key + hq * HEAD_DIM, HEAD_DIM)]
                kb = k.astype(BF16)
                prods = _dot_nt(jnp.concatenate([kb, q.astype(BF16)], axis=0),
                                jnp.concatenate([kb, kb], axis=0))
                heads = (2 * hq, 2 * hq + 1)
                g_full = [jnp.broadcast_to(g_cols[:, hv:hv + 1], (CHUNK, HEAD_DIM)) for hv in heads]
                b_full = [jnp.broadcast_to(betas[:, hv:hv + 1], (CHUNK, HEAD_DIM)) for hv in heads]
                g_col = jnp.where(left_half, g_full[0], g_full[1])
                b_col = jnp.where(left_half, b_full[0], b_full[1])
                g_row = jnp.concatenate([g_rows[hv:hv + 1, :] for hv in heads], axis=1)
                gamma = jnp.exp(jnp.where(incl, g_col - g_row, -jnp.inf))
                a_list.append(jnp.where(strict, prods[:CHUNK] * gamma * b_col, 0.0))
                ctx.append((cc, rows, hq, q, k, g_rows, g_row, g_full, b_full, prods[CHUNK:] * gamma))
        t_list = _unit_lower_inverse_minus_eye(a_list, diag_blocks, left_half)
        for t_m, (cc, rows, hq, q, k, g_rows, g_row, g_full, b_full, qk_gamma) in zip(t_list, ctx):
            heads = (2 * hq, 2 * hq + 1)
            k_t = k.T
            exp_g = [jnp.exp(gf) for gf in g_full]
            rhs = [jnp.concatenate([qkv_ref[rows, pl.ds(2 * key + hv * HEAD_DIM, HEAD_DIM)] * bf,
                                    k * (bf * eg)], axis=1)
                   for hv, bf, eg in zip(heads, b_full, exp_g)]
            rhs_both = jnp.concatenate(rhs, axis=0).astype(BF16)
            t_b = t_m.astype(BF16)
            zero = jnp.zeros_like(t_b)
            uw = [rhs[0] + _dot(jnp.where(left_half, t_b, zero), rhs_both),
                  rhs[1] + _dot(jnp.where(left_half, zero, t_b), rhs_both)]
            u_ref[cc, hq] = jnp.concatenate([uw[0][:, :HEAD_DIM], uw[1][:, :HEAD_DIM]], axis=1)
            lhs_a_ref[cc, hq, pl.ds(0, CHUNK), :] = jnp.concatenate(
                [uw[0][:, HEAD_DIM:], uw[1][:, HEAD_DIM:]], axis=1).astype(BF16)
            lhs_a_ref[cc, hq, pl.ds(CHUNK, CHUNK), :] = jnp.concatenate(
                [q * exp_g[0], q * exp_g[1]], axis=1).astype(BF16)
            g_last = jnp.where(left_row, g_rows[heads[0]:heads[0] + 1, CHUNK - 1:CHUNK],
                               g_rows[heads[1]:heads[1] + 1, CHUNK - 1:CHUNK])
            lhs_b_ref[cc, hq, pl.ds(0, CHUNK), :] = qk_gamma.astype(BF16)
            lhs_b_ref[cc, hq, pl.ds(CHUNK, HEAD_DIM), :] = (
                jnp.concatenate([k_t, k_t], axis=1) * jnp.exp(g_last - g_row)).astype(BF16)

    left_head = lax.broadcasted_iota(jnp.int32, (1, 2 * HEAD_DIM), 1) < HEAD_DIM

    def block_diag(pair):
        zero = jnp.zeros_like(pair)
        return jnp.concatenate([jnp.where(left_head, pair, zero), jnp.where(left_head, zero, pair)], axis=0)

    def recur_chunk(c, cc):
        rows = _chunk_rows(c)
        decays = jnp.exp(grow_ref[c][:, CHUNK - 1:CHUNK])
        states = [s_ref[hq] for hq in range(qk_heads)]
        ws_qs = [_dot(lhs_a_ref[cc, hq], block_diag(states[hq].astype(BF16))) for hq in range(qk_heads)]
        for hq in range(qk_heads):
            v_new = u_ref[cc, hq] - ws_qs[hq][:CHUNK]
            upd = _dot(lhs_b_ref[cc, hq], block_diag(v_new.astype(BF16)))
            o_pair = ws_qs[hq][CHUNK:] + upd[:CHUNK]
            decay = jnp.where(left_head, decays[2 * hq:2 * hq + 1, :], decays[2 * hq + 1:2 * hq + 2, :])
            s_ref[hq] = decay * states[hq] + upd[CHUNK:]
            for half in range(2):
                lanes = pl.ds((2 * hq + half) * HEAD_DIM, HEAD_DIM)
                o = o_pair[:, half * HEAD_DIM:(half + 1) * HEAD_DIM]
                y = o * lax.rsqrt(jnp.mean(o * o, axis=-1, keepdims=True) + NORM_EPS)
                y_ref[rows, lanes] = (y * ngain * jax.nn.silu(z_ref[rows, lanes])).astype(BF16)

    def chunk_group(gi, carry):
        prepare_chunks(gi)
        for cc in range(PREP_CHUNKS):
            recur_chunk(gi * PREP_CHUNKS + cc, cc)
        return carry

    if nchunk == PREP_CHUNKS:
        chunk_group(0, 0)
    else:
        lax.fori_loop(0, nchunk // PREP_CHUNKS, chunk_group, 0)
    m = _dot(y_ref[...], wout_ref[...])
    o_ref[0] = x_ref[0] + _rms(m, gpost_ref[...])


def _gdn_call(x, gpre, gpost, w_main, w_ba, conv_w, a_log, dt_bias, ngain, w_out, *, tt):
    bsz, t, d = x.shape
    vheads = a_log.shape[1]
    nconv = conv_w.shape[1]
    val = vheads * HEAD_DIM
    nchunk = tt // CHUNK
    const = lambda b, i: (0, 0)
    resident = dict(pipeline_mode=pl.Buffered(1))
    return pl.pallas_call(
        _gdn_body,
        grid=(bsz, t // tt),
        in_specs=[
            pl.BlockSpec((1, tt, d), lambda b, i: (b, i, 0)),
            pl.BlockSpec((1, d), const),
            pl.BlockSpec((1, d), const),
            pl.BlockSpec((d, nconv + val), const, **resident),
            pl.BlockSpec((d, 2 * vheads), const),
            pl.BlockSpec((CONV_K, nconv), const),
            pl.BlockSpec((1, vheads), const),
            pl.BlockSpec((1, vheads), const),
            pl.BlockSpec((1, HEAD_DIM), const),
            pl.BlockSpec((val, d), const, **resident),
        ],
        out_specs=pl.BlockSpec((1, tt, d), lambda b, i: (b, i, 0)),
        out_shape=jax.ShapeDtypeStruct((bsz, t, d), F32),
        scratch_shapes=[
            pltpu.VMEM((tt + CONV_HIST, PROJ_COL_GROUP), F32),
            pltpu.VMEM((CONV_HIST, nconv), F32),
            pltpu.VMEM((tt, nconv), F32),
            pltpu.VMEM((tt, val), F32),
            pltpu.VMEM((tt, val), BF16),
            pltpu.VMEM((vheads // 2, HEAD_DIM, 2 * HEAD_DIM), F32),
            pltpu.VMEM((tt, vheads), F32),
            pltpu.VMEM((nchunk, vheads, CHUNK), F32),
            pltpu.VMEM((tt, vheads), F32),
            pltpu.VMEM((PREP_CHUNKS, vheads // 2, CHUNK, 2 * HEAD_DIM), F32),
            pltpu.VMEM((PREP_CHUNKS, vheads // 2, 2 * CHUNK, 2 * HEAD_DIM), BF16),
            pltpu.VMEM((PREP_CHUNKS, vheads // 2, CHUNK + HEAD_DIM, 2 * CHUNK), BF16),
        ],
        compiler_params=pltpu.CompilerParams(
            dimension_semantics=("parallel", "arbitrary"),
            vmem_limit_bytes=V7X_VMEM_LIMIT_BYTES),
        name="gdn_sublayer",
    )(x, gpre, gpost, w_main, w_ba, conv_w, a_log, dt_bias, ngain, w_out)


def _tile(n, pref):
    t = min(n, pref)
    assert n % t == 0, (n, t)
    return t


def kernel(x, norm_gains, ffn1_w_in, ffn1_w_out, ffn2_w_in, ffn2_w_out, hg_w_in, hg_lower_bounds,
           hg_norm_gain, hg_w_out, gd_w_in, gd_conv_w, gd_a_log, gd_dt_bias, gd_norm_gain, gd_w_out):
    bsz, t, d = x.shape
    depth = norm_gains.shape[0]
    n_mixers = 2
    dff = ffn1_w_out.shape[1]
    nconv = gd_conv_w.shape[2]
    vheads = gd_a_log.shape[1]
    n_main = nconv + vheads * HEAD_DIM

    tm = _tile(bsz * t, 1024)
    assert dff % FFN_COL_BLOCK == 0
    tt_hg = _tile(t, 512)
    tt_gd = _tile(t, 256)

    def row(v):
        return v.reshape(1, -1)

    def ffn(xc, gpre, gpost, w_in, w_out):
        y = _ffn_call(xc.reshape(bsz * t, d), row(gpre), row(gpost),
                      w_in.astype(BF16), w_out.astype(BF16), tm=tm, nsub=4)
        return y.reshape(bsz, t, d)

    for i in range(depth):
        ng = norm_gains[i]
        j = i // n_mixers
        x = ffn(x, ng[0], ng[1], ffn1_w_in[i], ffn1_w_out[i])
        if i % n_mixers == 0:
            x = _hgrn2_call(x, row(ng[2]), row(ng[3]), hg_w_in[j].astype(BF16), hg_lower_bounds,
                            row(hg_norm_gain[j]), hg_w_out[j].astype(BF16), layer=j, tt=tt_hg)
        else:
            w_in = gd_w_in[j].astype(BF16)
            x = _gdn_call(x, row(ng[2]), row(ng[3]), w_in[:, :n_main], w_in[:, n_main:],
                          gd_conv_w[j], row(gd_a_log[j]), row(gd_dt_bias[j]),
                          row(gd_norm_gain[j]), gd_w_out[j].astype(BF16), tt=tt_gd)
        x = ffn(x, ng[4], ng[5], ffn2_w_in[i], ffn2_w_out[i])
    return x
```

```python
import functools

import jax
import jax.numpy as jnp
from jax import lax
from jax.experimental import pallas as pl
from jax.experimental.pallas import tpu as pltpu

NORM_EPS = 1e-6
CHUNK = 64
HEAD_DIM = 128
CONV_K = 4
CONV_HIST = 8
INV_BLOCK = 16
FFN_COL_BLOCK = 256
PROJ_COL_GROUP = 1024
PREP_CHUNKS = 4
V7X_VMEM_LIMIT_BYTES = 58 * 1024 * 1024

F32 = jnp.float32
BF16 = jnp.bfloat16
HI = lax.Precision.HIGHEST


def _dot(a, b, precision=None):
    return jnp.dot(a, b, preferred_element_type=F32, precision=precision)


def _dot_nt(a, b, precision=None):
    return lax.dot_general(a, b, (((1,), (1,)), ((), ())),
                           preferred_element_type=F32, precision=precision)


def _dot_tn(a, b, precision=None):
    return lax.dot_general(a, b, (((0,), (0,)), ((), ())),
                           preferred_element_type=F32, precision=precision)


def _dot_zero_one_lhs(a, x):
    a = a.astype(BF16)
    hi = x.astype(BF16)
    rest = x - hi.astype(F32)
    mid = rest.astype(BF16)
    lo = (rest - mid.astype(F32)).astype(BF16)
    return _dot(a, hi) + _dot(a, mid) + _dot(a, lo)


def _chunk_rows(c):
    start = c * CHUNK
    return pl.ds(start if isinstance(start, int) else pl.multiple_of(start, CHUNK), CHUNK)


def _rms(x, gain):
    return x * lax.rsqrt(jnp.mean(x * x, axis=-1, keepdims=True) + NORM_EPS) * gain


def _tril_mask(n, k=0):
    r = lax.broadcasted_iota(jnp.int32, (n, n), 0)
    c = lax.broadcasted_iota(jnp.int32, (n, n), 1)
    return c <= r + k


def _ffn_body(x_ref, xnext_ref, gpre_ref, gpost_ref, wi_ref, wo_ref, o_ref, act_ref, h0_ref):
    nsub, ts, dff = act_ref.shape
    nblk = dff // FFN_COL_BLOCK
    gpre = gpre_ref[...]
    half_gpost = 0.5 * gpost_ref[...]

    def pre_norm(s):
        return _rms(x_ref[pl.ds(s * ts, ts), :], gpre).astype(BF16)

    @pl.when(pl.program_id(0) == 0)
    def _():
        h0_ref[...] = pre_norm(0)

    def hidden_block(s, h, jb):
        cols = slice(jb * FFN_COL_BLOCK, (jb + 1) * FFN_COL_BLOCK)
        up_cols = slice(dff + jb * FFN_COL_BLOCK, dff + (jb + 1) * FFN_COL_BLOCK)
        gate = _dot(h, wi_ref[:, cols])
        up = _dot(h, wi_ref[:, up_cols])
        act_ref[s, :, cols] = (jax.nn.silu(gate) * up).astype(BF16)

    def finish(s):
        rows = pl.ds(s * ts, ts)
        y = _dot(act_ref[s], wo_ref[...])
        o_ref[rows, :] = x_ref[rows, :] + _rms(y, half_gpost)

    h = h0_ref[...]
    for s in range(nsub):
        h_next = None
        for jb in range(nblk):
            hidden_block(s, h, jb)
            if jb == 0 and s + 1 < nsub:
                h_next = pre_norm(s + 1)
            if jb == 0 and s + 1 == nsub:
                h0_ref[...] = _rms(xnext_ref[...], gpre).astype(BF16)
        if s > 0:
            finish(s - 1)
        h = h_next
    finish(nsub - 1)


def _ffn_call(x2, gpre, gpost, w_in, w_out, *, layer, tm, nsub):
    n, d = x2.shape
    dff = w_out.shape[1]
    ts = tm // nsub
    const = lambda i: (0, 0)
    this_layer = lambda i: (layer, 0, 0)
    resident = dict(pipeline_mode=pl.Buffered(1))
    return pl.pallas_call(
        _ffn_body,
        grid=(n // tm,),
        in_specs=[
            pl.BlockSpec((tm, d), lambda i: (i, 0)),
            pl.BlockSpec((ts, d), lambda i: (jnp.minimum((i + 1) * nsub, n // ts - 1), 0)),
            pl.BlockSpec((1, d), const),
            pl.BlockSpec((1, d), const),
            pl.BlockSpec((None, d, 2 * dff), this_layer, **resident),
            pl.BlockSpec((None, dff, d), this_layer, **resident),
        ],
        out_specs=pl.BlockSpec((tm, d), lambda i: (i, 0)),
        out_shape=jax.ShapeDtypeStruct((n, d), F32),
        scratch_shapes=[pltpu.VMEM((nsub, ts, dff), BF16), pltpu.VMEM((ts, d), BF16)],
        compiler_params=pltpu.CompilerParams(
            dimension_semantics=("arbitrary",),
            vmem_limit_bytes=V7X_VMEM_LIMIT_BYTES),
        name="ffn_sublayer",
    )(x2, x2, gpre, gpost, w_in, w_out)


def _hgrn2_body(layer, x_ref, gpre_ref, gpost_ref, win_ref, lbraw_ref, ngain_ref, wout_ref,
                o_ref, proj_ref, y_ref, st_ref):
    tt, d = x_ref.shape[1], x_ref.shape[2]
    heads = d // HEAD_DIM

    @pl.when(pl.program_id(1) == 0)
    def _():
        st_ref[...] = jnp.zeros_like(st_ref)

    h = _rms(x_ref[0], gpre_ref[...]).astype(BF16)
    proj_ref[...] = _dot(h, win_ref[...])

    raw = lbraw_ref[...]
    e = jnp.exp(raw - jnp.max(raw, axis=0, keepdims=True))
    p = e / jnp.sum(e, axis=0, keepdims=True)
    first = p[0:1, :]
    cum = first
    for l in range(1, layer + 1):
        cum = cum + p[l:l + 1, :]
    lb = cum - first

    tril = _tril_mask(CHUNK).astype(F32)
    causal = _tril_mask(CHUNK)
    ngain = ngain_ref[...]

    def chunk_body(c, carry):
        rows = pl.ds(pl.multiple_of(c * CHUNK, CHUNK), CHUNK)
        q = jax.nn.silu(proj_ref[rows, pl.ds(0, d)])
        forget = lb + (1.0 - lb) * jax.nn.sigmoid(proj_ref[rows, pl.ds(d, d)])
        k = 1.0 - forget
        b = _dot_zero_one_lhs(tril, jnp.log(forget))
        b_mid = b[CHUNK // 2:CHUNK // 2 + 1, :]
        b_last = b[CHUNK - 1:CHUNK, :]
        q_intra = q * jnp.exp(b - b_mid)
        k_intra = k * jnp.exp(b_mid - b)
        q_inter = q * jnp.exp(b)
        k_state = k * jnp.exp(b_last - b)
        decay = jnp.exp(b_last)
        cols = [slice(hh * HEAD_DIM, (hh + 1) * HEAD_DIM) for hh in range(heads)]
        vs = [proj_ref[rows, pl.ds(2 * d + hh * HEAD_DIM, HEAD_DIM)].astype(BF16) for hh in range(heads)]
        sts = [st_ref[hh] for hh in range(heads)]
        q_intra, k_intra = q_intra.astype(BF16), k_intra.astype(BF16)
        q_inter, k_state = q_inter.astype(BF16), k_state.astype(BF16)
        scores = [_dot_nt(q_intra[:, cs], k_intra[:, cs]) for cs in cols]
        inter = [_dot_nt(q_inter[:, cs], st.astype(BF16)) for cs, st in zip(cols, sts)]
        update = [_dot_tn(v, k_state[:, cs]) for cs, v in zip(cols, vs)]
        for hh in range(heads):
            cs = cols[hh]
            o = _dot(jnp.where(causal, scores[hh], 0.0).astype(BF16), vs[hh]) + inter[hh]
            st_ref[hh] = sts[hh] * decay[:, cs] + update[hh]
            gate = proj_ref[rows, pl.ds(3 * d + hh * HEAD_DIM, HEAD_DIM)]
            y = o * lax.rsqrt(jnp.mean(o * o, axis=-1, keepdims=True) + NORM_EPS)
            y_ref[rows, pl.ds(hh * HEAD_DIM, HEAD_DIM)] = (y * ngain * jax.nn.silu(gate)).astype(BF16)
        return carry

    lax.fori_loop(0, tt // CHUNK, chunk_body, 0, unroll=True)
    m = _dot(y_ref[...], wout_ref[...])
    o_ref[0] = x_ref[0] + _rms(m, gpost_ref[...])


def _hgrn2_call(x, gpre, gpost, w_in, lb_raw, ngain, w_out, *, layer, tt):
    bsz, t, d = x.shape
    heads = d // HEAD_DIM
    nl = lb_raw.shape[0]
    const = lambda b, i: (0, 0)
    this_layer = lambda b, i: (layer, 0, 0)
    return pl.pallas_call(
        functools.partial(_hgrn2_body, layer),
        grid=(bsz, t // tt),
        in_specs=[
            pl.BlockSpec((1, tt, d), lambda b, i: (b, i, 0)),
            pl.BlockSpec((1, d), const),
            pl.BlockSpec((1, d), const),
            pl.BlockSpec((None, d, 4 * d), this_layer),
            pl.BlockSpec((nl, d), const),
            pl.BlockSpec((1, HEAD_DIM), const),
            pl.BlockSpec((None, d, d), this_layer),
        ],
        out_specs=pl.BlockSpec((1, tt, d), lambda b, i: (b, i, 0)),
        out_shape=jax.ShapeDtypeStruct((bsz, t, d), F32),
        scratch_shapes=[
            pltpu.VMEM((tt, 4 * d), F32),
            pltpu.VMEM((tt, d), BF16),
            pltpu.VMEM((heads, HEAD_DIM, HEAD_DIM), F32),
        ],
        compiler_params=pltpu.CompilerParams(
            dimension_semantics=("parallel", "arbitrary"),
            vmem_limit_bytes=V7X_VMEM_LIMIT_BYTES),
        name="hgrn2_sublayer",
    )(x, gpre, gpost, w_in, lb_raw, ngain, w_out)


def _packed_matmul(xs, ys, left_half):
    out = []
    for x, y in zip(xs, ys):
        yb = y.astype(BF16)
        zero = jnp.zeros_like(yb)
        block_diag = jnp.concatenate([jnp.where(left_half, yb, zero), jnp.where(left_half, zero, yb)], axis=0)
        out.append(_dot(x.astype(BF16), block_diag))
    return out


def _unit_lower_inverse_minus_eye(a_list, diag_blocks, left_half):
    mm = functools.partial(_packed_matmul, left_half=left_half)

    dg = [jnp.where(diag_blocks, a, 0.0) for a in a_list]
    off = [a - d for a, d in zip(a_list, dg)]
    n = len(a_list)
    q = [-d for d in dg]
    p = mm(q, q)
    steps = INV_BLOCK.bit_length() - 2
    for step in range(steps):
        last = step == steps - 1
        prods = mm(p if last else p + p, q if last else q + p)
        q = [qi + pi + pqi for qi, pi, pqi in zip(q, p, prods[:n])]
        p = prods[n:]
    x = [-(o + qo) for o, qo in zip(off, mm(q, off))]
    x2 = mm(x, x)
    y = [xi + x2i + x3i for xi, x2i, x3i in zip(x, x2, mm(x, x2))]
    return [qi + yi + yqi for qi, yi, yqi in zip(q, y, mm(y, q))]


def _gdn_body(x_ref, gpre_ref, gpost_ref, win_ref, convw_ref, alog_ref, dtb_ref,
              ngain_ref, wout_ref, o_ref, ext_ref, hist_ref, qkv_ref, z_ref, y_ref, s_ref,
              gcol_ref, grow_ref, beta_ref, u_ref, lhs_a_ref, lhs_b_ref):
    tt, d = x_ref.shape[1], x_ref.shape[2]
    vheads = beta_ref.shape[1]
    nconv = qkv_ref.shape[1]
    key = (nconv - vheads * HEAD_DIM) // 2
    qk_heads = key // HEAD_DIM
    assert vheads == 2 * qk_heads
    assert key % PROJ_COL_GROUP == 0 and z_ref.shape[1] % PROJ_COL_GROUP == 0
    nchunk = tt // CHUNK
    t_idx = pl.program_id(1)

    @pl.when(t_idx == 0)
    def _():
        s_ref[...] = jnp.zeros_like(s_ref)
        hist_ref[...] = jnp.zeros_like(hist_ref)

    h = _rms(x_ref[0], gpre_ref[...]).astype(BF16)
    n_main = nconv + z_ref.shape[1]
    pba = _dot(h, win_ref[:, pl.ds(n_main, 2 * vheads)])
    beta_ref[...] = jax.nn.sigmoid(pba[:, :vheads])
    g = -jnp.exp(alog_ref[...]) * jax.nn.softplus(pba[:, vheads:] + dtb_ref[...])

    def project(c0):
        cols = pl.ds(c0, PROJ_COL_GROUP)
        pm = _dot(h, win_ref[:, cols])
        if c0 >= nconv:
            z_ref[:, pl.ds(c0 - nconv, PROJ_COL_GROUP)] = pm
            return
        ext_ref[pl.ds(0, CONV_HIST), :] = hist_ref[:, cols]
        ext_ref[pl.ds(CONV_HIST, tt), :] = pm
        hist_ref[:, cols] = pm[tt - CONV_HIST:, :]
        xe = ext_ref[...]
        acc = convw_ref[0:1, cols] * xe
        for j in range(1, CONV_K):
            acc = convw_ref[j:j + 1, cols] * xe + pltpu.roll(acc, shift=1, axis=0)
        act = jax.nn.silu(acc[CONV_HIST:, :])
        if c0 >= 2 * key:
            qkv_ref[:, cols] = act
            return
        for hh in range(PROJ_COL_GROUP // HEAD_DIM):
            a = act[:, hh * HEAD_DIM:(hh + 1) * HEAD_DIM]
            a = a * lax.rsqrt(jnp.sum(a * a, axis=-1, keepdims=True) + NORM_EPS)
            if c0 < key:
                a = a * (HEAD_DIM ** -0.5)
            qkv_ref[:, pl.ds(c0 + hh * HEAD_DIM, HEAD_DIM)] = a

    for c0 in range(0, n_main, PROJ_COL_GROUP):
        project(c0)

    tril = _tril_mask(CHUNK).astype(F32)
    r16 = lax.broadcasted_iota(jnp.int32, (vheads, vheads), 0)
    c16 = lax.broadcasted_iota(jnp.int32, (vheads, vheads), 1)
    eye_h = (r16 == c16).astype(F32)
    for c in range(nchunk):
        gc = g[c * CHUNK:(c + 1) * CHUNK, :]
        gcol_ref[pl.ds(c * CHUNK, CHUNK), :] = _dot(tril, gc, precision=HI)
        g_t = _dot_nt(eye_h, gc, precision=HI)
        grow_ref[c] = _dot_nt(g_t, tril, precision=HI)

    row = lax.broadcasted_iota(jnp.int32, (CHUNK, 2 * CHUNK), 0)
    lane = lax.broadcasted_iota(jnp.int32, (CHUNK, 2 * CHUNK), 1)
    col = lane % CHUNK
    left_half = lane < CHUNK
    incl = col <= row
    strict = col < row
    diag_blocks = (row // INV_BLOCK) == (col // INV_BLOCK)
    left_row = lax.broadcasted_iota(jnp.int32, (1, 2 * CHUNK), 1) < CHUNK
    ngain = ngain_ref[...]

    def prepare_chunks(ci):
        a_list, ctx = [], []
        for cc in range(PREP_CHUNKS):
            c = ci * PREP_CHUNKS + cc
            rows = _chunk_rows(c)
            g_cols = gcol_ref[rows, :]
            g_rows = grow_ref[c]
            betas = beta_ref[rows, :]
            for hq in range(qk_heads):
                q = qkv_ref[rows, pl.ds(hq * HEAD_DIM, HEAD_DIM)]
                k = qkv_ref[rows, pl.ds(key + hq * HEAD_DIM, HEAD_DIM)]
                kb = k.astype(BF16)
                prods = _dot_nt(jnp.concatenate([kb, q.astype(BF16)], axis=0),
                                jnp.concatenate([kb, kb], axis=0))
                heads = (2 * hq, 2 * hq + 1)
                g_full = [jnp.broadcast_to(g_cols[:, hv:hv + 1], (CHUNK, HEAD_DIM)) for hv in heads]
                b_full = [jnp.broadcast_to(betas[:, hv:hv + 1], (CHUNK, HEAD_DIM)) for hv in heads]
                g_col = jnp.where(left_half, g_full[0], g_full[1])
                b_col = jnp.where(left_half, b_full[0], b_full[1])
                g_row = jnp.concatenate([g_rows[hv:hv + 1, :] for hv in heads], axis=1)
                gamma = jnp.exp(jnp.where(incl, g_col - g_row, -jnp.inf))
                a_list.append(jnp.where(strict, prods[:CHUNK] * gamma * b_col, 0.0))
                ctx.append((cc, rows, hq, q, k, g_rows, g_row, g_full, b_full, prods[CHUNK:] * gamma))
        t_list = _unit_lower_inverse_minus_eye(a_list, diag_blocks, left_half)
        for t_m, (cc, rows, hq, q, k, g_rows, g_row, g_full, b_full, qk_gamma) in zip(t_list, ctx):
            heads = (2 * hq, 2 * hq + 1)
            k_t = k.T
            exp_g = [jnp.exp(gf) for gf in g_full]
            rhs = [jnp.concatenate([qkv_ref[rows, pl.ds(2 * key + hv * HEAD_DIM, HEAD_DIM)] * bf,
                                    k * (bf * eg)], axis=1)
                   for hv, bf, eg in zip(heads, b_full, exp_g)]
            rhs_both = jnp.concatenate(rhs, axis=0).astype(BF16)
            t_b = t_m.astype(BF16)
            zero = jnp.zeros_like(t_b)
            uw = [rhs[0] + _dot(jnp.where(left_half, t_b, zero), rhs_both),
                  rhs[1] + _dot(jnp.where(left_half, zero, t_b), rhs_both)]
            u_ref[cc, hq] = jnp.concatenate([uw[0][:, :HEAD_DIM], uw[1][:, :HEAD_DIM]], axis=1)
            lhs_a_ref[cc, hq, pl.ds(0, CHUNK), :] = jnp.concatenate(
                [uw[0][:, HEAD_DIM:], uw[1][:, HEAD_DIM:]], axis=1).astype(BF16)
            lhs_a_ref[cc, hq, pl.ds(CHUNK, CHUNK), :] = jnp.concatenate(
                [q * exp_g[0], q * exp_g[1]], axis=1).astype(BF16)
            g_last = jnp.where(left_row, g_rows[heads[0]:heads[0] + 1, CHUNK - 1:CHUNK],
                               g_rows[heads[1]:heads[1] + 1, CHUNK - 1:CHUNK])
            lhs_b_ref[cc, hq, pl.ds(0, CHUNK), :] = qk_gamma.astype(BF16)
            lhs_b_ref[cc, hq, pl.ds(CHUNK, HEAD_DIM), :] = (
                jnp.concatenate([k_t, k_t], axis=1) * jnp.exp(g_last - g_row)).astype(BF16)

    left_head = lax.broadcasted_iota(jnp.int32, (1, 2 * HEAD_DIM), 1) < HEAD_DIM

    def block_diag(pair):
        zero = jnp.zeros_like(pair)
        return jnp.concatenate([jnp.where(left_head, pair, zero), jnp.where(left_head, zero, pair)], axis=0)

    def recur_chunk(c, cc):
        rows = _chunk_rows(c)
        decays = jnp.exp(grow_ref[c][:, CHUNK - 1:CHUNK])
        states = [s_ref[hq] for hq in range(qk_heads)]
        ws_qs = [_dot(lhs_a_ref[cc, hq], block_diag(states[hq].astype(BF16))) for hq in range(qk_heads)]
        for hq in range(qk_heads):
            v_new = u_ref[cc, hq] - ws_qs[hq][:CHUNK]
            upd = _dot(lhs_b_ref[cc, hq], block_diag(v_new.astype(BF16)))
            o_pair = ws_qs[hq][CHUNK:] + upd[:CHUNK]
            decay = jnp.where(left_head, decays[2 * hq:2 * hq + 1, :], decays[2 * hq + 1:2 * hq + 2, :])
            s_ref[hq] = decay * states[hq] + upd[CHUNK:]
            for half in range(2):
                lanes = pl.ds((2 * hq + half) * HEAD_DIM, HEAD_DIM)
                o = o_pair[:, half * HEAD_DIM:(half + 1) * HEAD_DIM]
                y = o * lax.rsqrt(jnp.mean(o * o, axis=-1, keepdims=True) + NORM_EPS)
                y_ref[rows, lanes] = (y * ngain * jax.nn.silu(z_ref[rows, lanes])).astype(BF16)

    def chunk_group(gi, carry):
        prepare_chunks(gi)
        for cc in range(PREP_CHUNKS):
            recur_chunk(gi * PREP_CHUNKS + cc, cc)
        return carry

    if nchunk == PREP_CHUNKS:
        chunk_group(0, 0)
    else:
        lax.fori_loop(0, nchunk // PREP_CHUNKS, chunk_group, 0)
    m = _dot(y_ref[...], wout_ref[...])
    o_ref[0] = x_ref[0] + _rms(m, gpost_ref[...])


def _gdn_call(x, gpre, gpost, w_in, conv_w, a_log, dt_bias, ngain, w_out, *, layer, tt):
    bsz, t, d = x.shape
    vheads = a_log.shape[1]
    nconv = conv_w.shape[1]
    val = vheads * HEAD_DIM
    nchunk = tt // CHUNK
    const = lambda b, i: (0, 0)
    this_layer = lambda b, i: (layer, 0, 0)
    resident = dict(pipeline_mode=pl.Buffered(1))
    return pl.pallas_call(
        _gdn_body,
        grid=(bsz, t // tt),
        in_specs=[
            pl.BlockSpec((1, tt, d), lambda b, i: (b, i, 0)),
            pl.BlockSpec((1, d), const),
            pl.BlockSpec((1, d), const),
            pl.BlockSpec((None, d, w_in.shape[2]), this_layer, **resident),
            pl.BlockSpec((CONV_K, nconv), const),
            pl.BlockSpec((1, vheads), const),
            pl.BlockSpec((1, vheads), const),
            pl.BlockSpec((1, HEAD_DIM), const),
            pl.BlockSpec((None, val, d), this_layer, **resident),
        ],
        out_specs=pl.BlockSpec((1, tt, d), lambda b, i: (b, i, 0)),
        out_shape=jax.ShapeDtypeStruct((bsz, t, d), F32),
        scratch_shapes=[
            pltpu.VMEM((tt + CONV_HIST, PROJ_COL_GROUP), F32),
            pltpu.VMEM((CONV_HIST, nconv), F32),
            pltpu.VMEM((tt, nconv), F32),
            pltpu.VMEM((tt, val), F32),
            pltpu.VMEM((tt, val), BF16),
            pltpu.VMEM((vheads // 2, HEAD_DIM, 2 * HEAD_DIM), F32),
            pltpu.VMEM((tt, vheads), F32),
            pltpu.VMEM((nchunk, vheads, CHUNK), F32),
            pltpu.VMEM((tt, vheads), F32),
            pltpu.VMEM((PREP_CHUNKS, vheads // 2, CHUNK, 2 * HEAD_DIM), F32),
            pltpu.VMEM((PREP_CHUNKS, vheads // 2, 2 * CHUNK, 2 * HEAD_DIM), BF16),
            pltpu.VMEM((PREP_CHUNKS, vheads // 2, CHUNK + HEAD_DIM, 2 * CHUNK), BF16),
        ],
        compiler_params=pltpu.CompilerParams(
            dimension_semantics=("parallel", "arbitrary"),
            vmem_limit_bytes=V7X_VMEM_LIMIT_BYTES),
        name="gdn_sublayer",
    )(x, gpre, gpost, w_in, conv_w, a_log, dt_bias, ngain, w_out)


def _tile(n, pref):
    t = min(n, pref)
    assert n % t == 0, (n, t)
    return t


def kernel(x, norm_gains, ffn1_w_in, ffn1_w_out, ffn2_w_in, ffn2_w_out, hg_w_in, hg_lower_bounds,
           hg_norm_gain, hg_w_out, gd_w_in, gd_conv_w, gd_a_log, gd_dt_bias, gd_norm_gain, gd_w_out):
    bsz, t, d = x.shape
    depth = norm_gains.shape[0]
    n_mixers = 2
    dff = ffn1_w_out.shape[1]

    tm = _tile(bsz * t, 1024)
    assert dff % FFN_COL_BLOCK == 0
    tt_hg = _tile(t, 512)
    tt_gd = _tile(t, 256)

    def row(v):
        return v.reshape(1, -1)

    ffn1_w_in, ffn1_w_out, ffn2_w_in, ffn2_w_out, hg_w_in, hg_w_out, gd_w_in, gd_w_out = (
        w.astype(BF16) for w in (ffn1_w_in, ffn1_w_out, ffn2_w_in, ffn2_w_out, hg_w_in, hg_w_out, gd_w_in, gd_w_out))

    def ffn(xc, gpre, gpost, w_in, w_out, layer):
        y = _ffn_call(xc.reshape(bsz * t, d), row(gpre), row(gpost), w_in, w_out, layer=layer, tm=tm, nsub=4)
        return y.reshape(bsz, t, d)

    for i in range(depth):
        ng = norm_gains[i]
        j = i // n_mixers
        x = ffn(x, ng[0], ng[1], ffn1_w_in, ffn1_w_out, i)
        if i % n_mixers == 0:
            x = _hgrn2_call(x, row(ng[2]), row(ng[3]), hg_w_in, hg_lower_bounds,
                            row(hg_norm_gain[j]), hg_w_out, layer=j, tt=tt_hg)
        else:
            x = _gdn_call(x, row(ng[2]), row(ng[3]), gd_w_in, gd_conv_w[j], row(gd_a_log[j]),
                          row(gd_dt_bias[j]), row(gd_norm_gain[j]), gd_w_out, layer=j, tt=tt_gd)
        x = ffn(x, ng[4], ng[5], ffn2_w_in, ffn2_w_out, i)
    return x
```

```python
import functools

import jax
import jax.numpy as jnp
from jax import lax
from jax.experimental import pallas as pl
from jax.experimental.pallas import tpu as pltpu

NORM_EPS = 1e-6
CHUNK = 64
HEAD_DIM = 128
CONV_K = 4
CONV_HIST = 8
INV_BLOCK = 16
FFN_COL_BLOCK = 256
PROJ_COL_GROUP = 1024
PREP_CHUNKS = 4
V7X_VMEM_LIMIT_BYTES = 58 * 1024 * 1024

F32 = jnp.float32
BF16 = jnp.bfloat16
HI = lax.Precision.HIGHEST


def _dot(a, b, precision=None):
    return jnp.dot(a, b, preferred_element_type=F32, precision=precision)


def _dot_nt(a, b, precision=None):
    return lax.dot_general(a, b, (((1,), (1,)), ((), ())),
                           preferred_element_type=F32, precision=precision)


def _dot_tn(a, b, precision=None):
    return lax.dot_general(a, b, (((0,), (0,)), ((), ())),
                           preferred_element_type=F32, precision=precision)


def _cumsum_rows(x, masks):
    for k, mask in enumerate(masks):
        x = x + jnp.where(mask, pltpu.roll(x, shift=1 << k, axis=0), 0.0)
    return x


def _chunk_rows(c):
    start = c * CHUNK
    return pl.ds(start if isinstance(start, int) else pl.multiple_of(start, CHUNK), CHUNK)


def _rms(x, gain):
    return x * lax.rsqrt(jnp.mean(x * x, axis=-1, keepdims=True) + NORM_EPS) * gain


def _tril_mask(n, k=0):
    r = lax.broadcasted_iota(jnp.int32, (n, n), 0)
    c = lax.broadcasted_iota(jnp.int32, (n, n), 1)
    return c <= r + k


def _ffn_body(x_ref, xnext_ref, gpre_ref, gpost_ref, wi_ref, wo_ref, o_ref, act_ref, h0_ref):
    nsub, ts, dff = act_ref.shape
    nblk = dff // FFN_COL_BLOCK
    gpre = gpre_ref[...]
    half_gpost = 0.5 * gpost_ref[...]

    def pre_norm(s):
        return _rms(x_ref[pl.ds(s * ts, ts), :], gpre).astype(BF16)

    @pl.when(pl.program_id(0) == 0)
    def _():
        h0_ref[...] = pre_norm(0)

    def hidden_block(s, h, jb):
        cols = slice(jb * FFN_COL_BLOCK, (jb + 1) * FFN_COL_BLOCK)
        up_cols = slice(dff + jb * FFN_COL_BLOCK, dff + (jb + 1) * FFN_COL_BLOCK)
        gate = _dot(h, wi_ref[:, cols])
        up = _dot(h, wi_ref[:, up_cols])
        act_ref[s, :, cols] = (jax.nn.silu(gate) * up).astype(BF16)

    def finish(s):
        rows = pl.ds(s * ts, ts)
        y = _dot(act_ref[s], wo_ref[...])
        o_ref[rows, :] = x_ref[rows, :] + _rms(y, half_gpost)

    h = h0_ref[...]
    for s in range(nsub):
        h_next = None
        for jb in range(nblk):
            hidden_block(s, h, jb)
            if jb == 0 and s + 1 < nsub:
                h_next = pre_norm(s + 1)
            if jb == 0 and s + 1 == nsub:
                h0_ref[...] = _rms(xnext_ref[...], gpre).astype(BF16)
        if s > 0:
            finish(s - 1)
        h = h_next
    finish(nsub - 1)


def _ffn_call(x2, gpre, gpost, w_in, w_out, *, layer, tm, nsub):
    n, d = x2.shape
    dff = w_out.shape[1]
    ts = tm // nsub
    const = lambda i: (0, 0)
    this_layer = lambda i: (layer, 0, 0)
    resident = dict(pipeline_mode=pl.Buffered(1))
    return pl.pallas_call(
        _ffn_body,
        grid=(n // tm,),
        in_specs=[
            pl.BlockSpec((tm, d), lambda i: (i, 0)),
            pl.BlockSpec((ts, d), lambda i: (jnp.minimum((i + 1) * nsub, n // ts - 1), 0)),
            pl.BlockSpec((1, d), const),
            pl.BlockSpec((1, d), const),
            pl.BlockSpec((None, d, 2 * dff), this_layer, **resident),
            pl.BlockSpec((None, dff, d), this_layer, **resident),
        ],
        out_specs=pl.BlockSpec((tm, d), lambda i: (i, 0)),
        out_shape=jax.ShapeDtypeStruct((n, d), F32),
        scratch_shapes=[pltpu.VMEM((nsub, ts, dff), BF16), pltpu.VMEM((ts, d), BF16)],
        compiler_params=pltpu.CompilerParams(
            dimension_semantics=("arbitrary",),
            vmem_limit_bytes=V7X_VMEM_LIMIT_BYTES),
        name="ffn_sublayer",
    )(x2, x2, gpre, gpost, w_in, w_out)


def _hgrn2_body(layer, x_ref, gpre_ref, gpost_ref, win_ref, lbraw_ref, ngain_ref, wout_ref,
                o_ref, proj_ref, y_ref, st_ref):
    tt, d = x_ref.shape[1], x_ref.shape[2]
    heads = d // HEAD_DIM

    @pl.when(pl.program_id(1) == 0)
    def _():
        st_ref[...] = jnp.zeros_like(st_ref)

    h = _rms(x_ref[0], gpre_ref[...]).astype(BF16)
    proj_ref[...] = _dot(h, win_ref[...])

    raw = lbraw_ref[...]
    e = jnp.exp(raw - jnp.max(raw, axis=0, keepdims=True))
    p = e / jnp.sum(e, axis=0, keepdims=True)
    first = p[0:1, :]
    cum = first
    for l in range(1, layer + 1):
        cum = cum + p[l:l + 1, :]
    lb = cum - first

    causal = _tril_mask(CHUNK)
    chunk_row = lax.broadcasted_iota(jnp.int32, (CHUNK, d), 0)
    scan_masks = [chunk_row >= (1 << k) for k in range(CHUNK.bit_length() - 1)]
    ngain = ngain_ref[...]

    def chunk_body(c, carry):
        rows = pl.ds(pl.multiple_of(c * CHUNK, CHUNK), CHUNK)
        q = jax.nn.silu(proj_ref[rows, pl.ds(0, d)])
        forget = lb + (1.0 - lb) * jax.nn.sigmoid(proj_ref[rows, pl.ds(d, d)])
        k = 1.0 - forget
        b = _cumsum_rows(jnp.log(forget), scan_masks)
        b_mid = b[CHUNK // 2:CHUNK // 2 + 1, :]
        b_last = b[CHUNK - 1:CHUNK, :]
        q_intra = q * jnp.exp(b - b_mid)
        k_intra = k * jnp.exp(b_mid - b)
        q_inter = q * jnp.exp(b)
        k_state = k * jnp.exp(b_last - b)
        decay = jnp.exp(b_last)
        cols = [slice(hh * HEAD_DIM, (hh + 1) * HEAD_DIM) for hh in range(heads)]
        vs = [proj_ref[rows, pl.ds(2 * d + hh * HEAD_DIM, HEAD_DIM)].astype(BF16) for hh in range(heads)]
        sts = [st_ref[hh] for hh in range(heads)]
        q_intra, k_intra = q_intra.astype(BF16), k_intra.astype(BF16)
        q_inter, k_state = q_inter.astype(BF16), k_state.astype(BF16)
        scores = [_dot_nt(q_intra[:, cs], k_intra[:, cs]) for cs in cols]
        inter = [_dot_nt(q_inter[:, cs], st.astype(BF16)) for cs, st in zip(cols, sts)]
        update = [_dot_tn(v, k_state[:, cs]) for cs, v in zip(cols, vs)]
        for hh in range(heads):
            cs = cols[hh]
            o = _dot(jnp.where(causal, scores[hh], 0.0).astype(BF16), vs[hh]) + inter[hh]
            st_ref[hh] = sts[hh] * decay[:, cs] + update[hh]
            gate = proj_ref[rows, pl.ds(3 * d + hh * HEAD_DIM, HEAD_DIM)]
            y = o * lax.rsqrt(jnp.mean(o * o, axis=-1, keepdims=True) + NORM_EPS)
            y_ref[rows, pl.ds(hh * HEAD_DIM, HEAD_DIM)] = (y * ngain * jax.nn.silu(gate)).astype(BF16)
        return carry

    lax.fori_loop(0, tt // CHUNK, chunk_body, 0, unroll=True)
    m = _dot(y_ref[...], wout_ref[...])
    o_ref[0] = x_ref[0] + _rms(m, gpost_ref[...])


def _hgrn2_call(x, gpre, gpost, w_in, lb_raw, ngain, w_out, *, layer, tt):
    bsz, t, d = x.shape
    heads = d // HEAD_DIM
    nl = lb_raw.shape[0]
    const = lambda b, i: (0, 0)
    this_layer = lambda b, i: (layer, 0, 0)
    return pl.pallas_call(
        functools.partial(_hgrn2_body, layer),
        grid=(bsz, t // tt),
        in_specs=[
            pl.BlockSpec((1, tt, d), lambda b, i: (b, i, 0)),
            pl.BlockSpec((1, d), const),
            pl.BlockSpec((1, d), const),
            pl.BlockSpec((None, d, 4 * d), this_layer),
            pl.BlockSpec((nl, d), const),
            pl.BlockSpec((1, HEAD_DIM), const),
            pl.BlockSpec((None, d, d), this_layer),
        ],
        out_specs=pl.BlockSpec((1, tt, d), lambda b, i: (b, i, 0)),
        out_shape=jax.ShapeDtypeStruct((bsz, t, d), F32),
        scratch_shapes=[
            pltpu.VMEM((tt, 4 * d), F32),
            pltpu.VMEM((tt, d), BF16),
            pltpu.VMEM((heads, HEAD_DIM, HEAD_DIM), F32),
        ],
        compiler_params=pltpu.CompilerParams(
            dimension_semantics=("parallel", "arbitrary"),
            vmem_limit_bytes=V7X_VMEM_LIMIT_BYTES),
        name="hgrn2_sublayer",
    )(x, gpre, gpost, w_in, lb_raw, ngain, w_out)


def _packed_matmul(xs, ys, left_half):
    out = []
    for x, y in zip(xs, ys):
        yb = y.astype(BF16)
        zero = jnp.zeros_like(yb)
        block_diag = jnp.concatenate([jnp.where(left_half, yb, zero), jnp.where(left_half, zero, yb)], axis=0)
        out.append(_dot(x.astype(BF16), block_diag))
    return out


def _unit_lower_inverse_minus_eye(a_list, diag_blocks, left_half):
    mm = functools.partial(_packed_matmul, left_half=left_half)

    dg = [jnp.where(diag_blocks, a, 0.0) for a in a_list]
    off = [a - d for a, d in zip(a_list, dg)]
    n = len(a_list)
    q = [-d for d in dg]
    p = mm(q, q)
    steps = INV_BLOCK.bit_length() - 2
    for step in range(steps):
        last = step == steps - 1
        prods = mm(p if last else p + p, q if last else q + p)
        q = [qi + pi + pqi for qi, pi, pqi in zip(q, p, prods[:n])]
        p = prods[n:]
    x = [-(o + qo) for o, qo in zip(off, mm(q, off))]
    x2 = mm(x, x)
    y = [xi + x2i + x3i for xi, x2i, x3i in zip(x, x2, mm(x, x2))]
    return [qi + yi + yqi for qi, yi, yqi in zip(q, y, mm(y, q))]


def _gdn_body(x_ref, gpre_ref, gpost_ref, win_ref, convw_ref, alog_ref, dtb_ref,
              ngain_ref, wout_ref, o_ref, ext_ref, hist_ref, qkv_ref, z_ref, y_ref, s_ref,
              gcol_ref, grow_ref, beta_ref, u_ref, lhs_a_ref, lhs_b_ref):
    tt, d = x_ref.shape[1], x_ref.shape[2]
    vheads = beta_ref.shape[1]
    nconv = qkv_ref.shape[1]
    key = (nconv - vheads * HEAD_DIM) // 2
    qk_heads = key // HEAD_DIM
    assert vheads == 2 * qk_heads
    assert key % PROJ_COL_GROUP == 0 and z_ref.shape[1] % PROJ_COL_GROUP == 0
    nchunk = tt // CHUNK
    t_idx = pl.program_id(1)

    @pl.when(t_idx == 0)
    def _():
        s_ref[...] = jnp.zeros_like(s_ref)
        hist_ref[...] = jnp.zeros_like(hist_ref)

    h = _rms(x_ref[0], gpre_ref[...]).astype(BF16)
    n_main = nconv + z_ref.shape[1]
    pba = _dot(h, win_ref[:, pl.ds(n_main, 2 * vheads)])
    beta_ref[...] = jax.nn.sigmoid(pba[:, :vheads])
    g = -jnp.exp(alog_ref[...]) * jax.nn.softplus(pba[:, vheads:] + dtb_ref[...])

    def project(c0):
        cols = pl.ds(c0, PROJ_COL_GROUP)
        pm = _dot(h, win_ref[:, cols])
        if c0 >= nconv:
            z_ref[:, pl.ds(c0 - nconv, PROJ_COL_GROUP)] = pm
            return
        ext_ref[pl.ds(0, CONV_HIST), :] = hist_ref[:, cols]
        ext_ref[pl.ds(CONV_HIST, tt), :] = pm
        hist_ref[:, cols] = pm[tt - CONV_HIST:, :]
        xe = ext_ref[...]
        acc = convw_ref[0:1, cols] * xe
        for j in range(1, CONV_K):
            acc = convw_ref[j:j + 1, cols] * xe + pltpu.roll(acc, shift=1, axis=0)
        act = jax.nn.silu(acc[CONV_HIST:, :])
        if c0 >= 2 * key:
            qkv_ref[:, cols] = act
            return
        for hh in range(PROJ_COL_GROUP // HEAD_DIM):
            a = act[:, hh * HEAD_DIM:(hh + 1) * HEAD_DIM]
            a = a * lax.rsqrt(jnp.sum(a * a, axis=-1, keepdims=True) + NORM_EPS)
            if c0 < key:
                a = a * (HEAD_DIM ** -0.5)
            qkv_ref[:, pl.ds(c0 + hh * HEAD_DIM, HEAD_DIM)] = a

    for c0 in range(0, n_main, PROJ_COL_GROUP):
        project(c0)

    tril = _tril_mask(CHUNK).astype(F32)
    r16 = lax.broadcasted_iota(jnp.int32, (vheads, vheads), 0)
    c16 = lax.broadcasted_iota(jnp.int32, (vheads, vheads), 1)
    eye_h = (r16 == c16).astype(F32)
    for c in range(nchunk):
        gc = g[c * CHUNK:(c + 1) * CHUNK, :]
        gcol_ref[pl.ds(c * CHUNK, CHUNK), :] = _dot(tril, gc, precision=HI)
        g_t = _dot_nt(eye_h, gc, precision=HI)
        grow_ref[c] = _dot_nt(g_t, tril, precision=HI)

    row = lax.broadcasted_iota(jnp.int32, (CHUNK, 2 * CHUNK), 0)
    lane = lax.broadcasted_iota(jnp.int32, (CHUNK, 2 * CHUNK), 1)
    col = lane % CHUNK
    left_half = lane < CHUNK
    incl = col <= row
    strict = col < row
    diag_blocks = (row // INV_BLOCK) == (col // INV_BLOCK)
    left_row = lax.broadcasted_iota(jnp.int32, (1, 2 * CHUNK), 1) < CHUNK
    ngain = ngain_ref[...]

    def prepare_chunks(ci):
        a_list, ctx = [], []
        for cc in range(PREP_CHUNKS):
            c = ci * PREP_CHUNKS + cc
            rows = _chunk_rows(c)
            g_cols = gcol_ref[rows, :]
            g_rows = grow_ref[c]
            betas = beta_ref[rows, :]
            for hq in range(qk_heads):
                q = qkv_ref[rows, pl.ds(hq * HEAD_DIM, HEAD_DIM)]
                k = qkv_ref[rows, pl.ds(key + hq * HEAD_DIM, HEAD_DIM)]
                kb = k.astype(BF16)
                prods = _dot_nt(jnp.concatenate([kb, q.astype(BF16)], axis=0),
                                jnp.concatenate([kb, kb], axis=0))
                heads = (2 * hq, 2 * hq + 1)
                g_full = [jnp.broadcast_to(g_cols[:, hv:hv + 1], (CHUNK, HEAD_DIM)) for hv in heads]
                b_full = [jnp.broadcast_to(betas[:, hv:hv + 1], (CHUNK, HEAD_DIM)) for hv in heads]
                g_col = jnp.where(left_half, g_full[0], g_full[1])
                b_col = jnp.where(left_half, b_full[0], b_full[1])
                g_row = jnp.concatenate([g_rows[hv:hv + 1, :] for hv in heads], axis=1)
                gamma = jnp.exp(jnp.where(incl, g_col - g_row, -jnp.inf))
                a_list.append(jnp.where(strict, prods[:CHUNK] * gamma * b_col, 0.0))
                ctx.append((cc, rows, hq, q, k, g_rows, g_row, g_full, b_full, prods[CHUNK:] * gamma))
        t_list = _unit_lower_inverse_minus_eye(a_list, diag_blocks, left_half)
        for t_m, (cc, rows, hq, q, k, g_rows, g_row, g_full, b_full, qk_gamma) in zip(t_list, ctx):
            heads = (2 * hq, 2 * hq + 1)
            k_t = k.T
            exp_g = [jnp.exp(gf) for gf in g_full]
            rhs = [jnp.concatenate([qkv_ref[rows, pl.ds(2 * key + hv * HEAD_DIM, HEAD_DIM)] * bf,
                                    k * (bf * eg)], axis=1)
                   for hv, bf, eg in zip(heads, b_full, exp_g)]
            rhs_both = jnp.concatenate(rhs, axis=0).astype(BF16)
            t_b = t_m.astype(BF16)
            zero = jnp.zeros_like(t_b)
            t_rhs = _dot(jnp.concatenate([jnp.where(left_half, t_b, zero), jnp.where(left_half, zero, t_b)], axis=0),
                         rhs_both)
            uw = [rhs[0] + t_rhs[:CHUNK], rhs[1] + t_rhs[CHUNK:]]
            u_ref[cc, hq] = jnp.concatenate([uw[0][:, :HEAD_DIM], uw[1][:, :HEAD_DIM]], axis=1)
            lhs_a_ref[cc, hq, pl.ds(0, CHUNK), :] = jnp.concatenate(
                [uw[0][:, HEAD_DIM:], uw[1][:, HEAD_DIM:]], axis=1).astype(BF16)
            lhs_a_ref[cc, hq, pl.ds(CHUNK, CHUNK), :] = jnp.concatenate(
                [q * exp_g[0], q * exp_g[1]], axis=1).astype(BF16)
            g_last = jnp.where(left_row, g_rows[heads[0]:heads[0] + 1, CHUNK - 1:CHUNK],
                               g_rows[heads[1]:heads[1] + 1, CHUNK - 1:CHUNK])
            lhs_b_ref[cc, hq, pl.ds(0, CHUNK), :] = qk_gamma.astype(BF16)
            lhs_b_ref[cc, hq, pl.ds(CHUNK, HEAD_DIM), :] = (
                jnp.concatenate([k_t, k_t], axis=1) * jnp.exp(g_last - g_row)).astype(BF16)

    left_head = lax.broadcasted_iota(jnp.int32, (1, 2 * HEAD_DIM), 1) < HEAD_DIM

    def block_diag(pair):
        zero = jnp.zeros_like(pair)
        return jnp.concatenate([jnp.where(left_head, pair, zero), jnp.where(left_head, zero, pair)], axis=0)

    def recur_chunk(c, cc):
        rows = _chunk_rows(c)
        decays = jnp.exp(grow_ref[c][:, CHUNK - 1:CHUNK])
        states = [s_ref[hq] for hq in range(qk_heads)]
        ws_qs = [_dot(lhs_a_ref[cc, hq], block_diag(states[hq].astype(BF16))) for hq in range(qk_heads)]
        for hq in range(qk_heads):
            v_new = u_ref[cc, hq] - ws_qs[hq][:CHUNK]
            upd = _dot(lhs_b_ref[cc, hq], block_diag(v_new.astype(BF16)))
            o_pair = ws_qs[hq][CHUNK:] + upd[:CHUNK]
            decay = jnp.where(left_head, decays[2 * hq:2 * hq + 1, :], decays[2 * hq + 1:2 * hq + 2, :])
            s_ref[hq] = decay * states[hq] + upd[CHUNK:]
            for half in range(2):
                lanes = pl.ds((2 * hq + half) * HEAD_DIM, HEAD_DIM)
                o = o_pair[:, half * HEAD_DIM:(half + 1) * HEAD_DIM]
                y = o * lax.rsqrt(jnp.mean(o * o, axis=-1, keepdims=True) + NORM_EPS)
                y_ref[rows, lanes] = (y * ngain * jax.nn.silu(z_ref[rows, lanes])).astype(BF16)

    def chunk_group(gi, carry):
        prepare_chunks(gi)
        for cc in range(PREP_CHUNKS):
            recur_chunk(gi * PREP_CHUNKS + cc, cc)
        return carry

    if nchunk == PREP_CHUNKS:
        chunk_group(0, 0)
    else:
        lax.fori_loop(0, nchunk // PREP_CHUNKS, chunk_group, 0)
    m = _dot(y_ref[...], wout_ref[...])
    o_ref[0] = x_ref[0] + _rms(m, gpost_ref[...])


def _gdn_call(x, gpre, gpost, w_in, conv_w, a_log, dt_bias, ngain, w_out, *, layer, tt):
    bsz, t, d = x.shape
    vheads = a_log.shape[1]
    nconv = conv_w.shape[1]
    val = vheads * HEAD_DIM
    nchunk = tt // CHUNK
    const = lambda b, i: (0, 0)
    this_layer = lambda b, i: (layer, 0, 0)
    resident = dict(pipeline_mode=pl.Buffered(1))
    return pl.pallas_call(
        _gdn_body,
        grid=(bsz, t // tt),
        in_specs=[
            pl.BlockSpec((1, tt, d), lambda b, i: (b, i, 0)),
            pl.BlockSpec((1, d), const),
            pl.BlockSpec((1, d), const),
            pl.BlockSpec((None, d, w_in.shape[2]), this_layer, **resident),
            pl.BlockSpec((CONV_K, nconv), const),
            pl.BlockSpec((1, vheads), const),
            pl.BlockSpec((1, vheads), const),
            pl.BlockSpec((1, HEAD_DIM), const),
            pl.BlockSpec((None, val, d), this_layer, **resident),
        ],
        out_specs=pl.BlockSpec((1, tt, d), lambda b, i: (b, i, 0)),
        out_shape=jax.ShapeDtypeStruct((bsz, t, d), F32),
        scratch_shapes=[
            pltpu.VMEM((tt + CONV_HIST, PROJ_COL_GROUP), F32),
            pltpu.VMEM((CONV_HIST, nconv), F32),
            pltpu.VMEM((tt, nconv), F32),
            pltpu.VMEM((tt, val), F32),
            pltpu.VMEM((tt, val), BF16),
            pltpu.VMEM((vheads // 2, HEAD_DIM, 2 * HEAD_DIM), F32),
            pltpu.VMEM((tt, vheads), F32),
            pltpu.VMEM((nchunk, vheads, CHUNK), F32),
            pltpu.VMEM((tt, vheads), F32),
            pltpu.VMEM((PREP_CHUNKS, vheads // 2, CHUNK, 2 * HEAD_DIM), F32),
            pltpu.VMEM((PREP_CHUNKS, vheads // 2, 2 * CHUNK, 2 * HEAD_DIM), BF16),
            pltpu.VMEM((PREP_CHUNKS, vheads // 2, CHUNK + HEAD_DIM, 2 * CHUNK), BF16),
        ],
        compiler_params=pltpu.CompilerParams(
            dimension_semantics=("parallel", "arbitrary"),
            vmem_limit_bytes=V7X_VMEM_LIMIT_BYTES),
        name="gdn_sublayer",
    )(x, gpre, gpost, w_in, conv_w, a_log, dt_bias, ngain, w_out)


def _tile(n, pref):
    t = min(n, pref)
    assert n % t == 0, (n, t)
    return t


def kernel(x, norm_gains, ffn1_w_in, ffn1_w_out, ffn2_w_in, ffn2_w_out, hg_w_in, hg_lower_bounds,
           hg_norm_gain, hg_w_out, gd_w_in, gd_conv_w, gd_a_log, gd_dt_bias, gd_norm_gain, gd_w_out):
    bsz, t, d = x.shape
    depth = norm_gains.shape[0]
    n_mixers = 2
    dff = ffn1_w_out.shape[1]

    tm = _tile(bsz * t, 1024)
    assert dff % FFN_COL_BLOCK == 0
    tt_hg = _tile(t, 512)
    tt_gd = _tile(t, 256)

    def row(v):
        return v.reshape(1, -1)

    ffn1_w_in, ffn1_w_out, ffn2_w_in, ffn2_w_out, hg_w_in, hg_w_out, gd_w_in, gd_w_out = (
        w.astype(BF16) for w in (ffn1_w_in, ffn1_w_out, ffn2_w_in, ffn2_w_out, hg_w_in, hg_w_out, gd_w_in, gd_w_out))

    def ffn(xc, gpre, gpost, w_in, w_out, layer):
        y = _ffn_call(xc.reshape(bsz * t, d), row(gpre), row(gpost), w_in, w_out, layer=layer, tm=tm, nsub=4)
        return y.reshape(bsz, t, d)

    for i in range(depth):
        ng = norm_gains[i]
        j = i // n_mixers
        x = ffn(x, ng[0], ng[1], ffn1_w_in, ffn1_w_out, i)
        if i % n_mixers == 0:
            x = _hgrn2_call(x, row(ng[2]), row(ng[3]), hg_w_in, hg_lower_bounds,
                            row(hg_norm_gain[j]), hg_w_out, layer=j, tt=tt_hg)
        else:
            x = _gdn_call(x, row(ng[2]), row(ng[3]), gd_w_in, gd_conv_w[j], row(gd_a_log[j]),
                          row(gd_dt_bias[j]), row(gd_norm_gain[j]), gd_w_out, layer=j, tt=tt_gd)
        x = ffn(x, ng[4], ng[5], ffn2_w_in, ffn2_w_out, i)
    return x
```

```python
import functools

import jax
import jax.numpy as jnp
from jax import lax
from jax.experimental import pallas as pl
from jax.experimental.pallas import tpu as pltpu

NORM_EPS = 1e-6
CHUNK = 64
HEAD_DIM = 128
CONV_K = 4
CONV_HIST = 8
INV_BLOCK = 16
FFN_COL_BLOCK = 256
PROJ_COL_GROUP = 1024
PREP_CHUNKS = 4
V7X_VMEM_LIMIT_BYTES = 58 * 1024 * 1024

F32 = jnp.float32
BF16 = jnp.bfloat16
HI = lax.Precision.HIGHEST


def _dot(a, b, precision=None):
    return jnp.dot(a, b, preferred_element_type=F32, precision=precision)


def _dot_nt(a, b, precision=None):
    return lax.dot_general(a, b, (((1,), (1,)), ((), ())),
                           preferred_element_type=F32, precision=precision)


def _dot_tn(a, b, precision=None):
    return lax.dot_general(a, b, (((0,), (0,)), ((), ())),
                           preferred_element_type=F32, precision=precision)


def _cumsum_rows(x, masks):
    for k, mask in enumerate(masks):
        x = x + jnp.where(mask, pltpu.roll(x, shift=1 << k, axis=0), 0.0)
    return x


def _chunk_rows(c):
    start = c * CHUNK
    return pl.ds(start if isinstance(start, int) else pl.multiple_of(start, CHUNK), CHUNK)


def _rms(x, gain):
    return x * lax.rsqrt(jnp.mean(x * x, axis=-1, keepdims=True) + NORM_EPS) * gain


def _tril_mask(n, k=0):
    r = lax.broadcasted_iota(jnp.int32, (n, n), 0)
    c = lax.broadcasted_iota(jnp.int32, (n, n), 1)
    return c <= r + k


def _ffn_body(x_ref, xnext_ref, gpre_ref, gpost_ref, wi_ref, wo_ref, o_ref, act_ref, h0_ref):
    nsub, ts, dff = act_ref.shape
    nblk = dff // FFN_COL_BLOCK
    gpre = gpre_ref[...]
    half_gpost = 0.5 * gpost_ref[...]

    def pre_norm(s):
        return _rms(x_ref[pl.ds(s * ts, ts), :], gpre).astype(BF16)

    @pl.when(pl.program_id(0) == 0)
    def _():
        h0_ref[...] = pre_norm(0)

    def hidden_block(s, h, jb):
        cols = slice(jb * FFN_COL_BLOCK, (jb + 1) * FFN_COL_BLOCK)
        up_cols = slice(dff + jb * FFN_COL_BLOCK, dff + (jb + 1) * FFN_COL_BLOCK)
        gate = _dot(h, wi_ref[:, cols])
        up = _dot(h, wi_ref[:, up_cols])
        act_ref[s, :, cols] = (jax.nn.silu(gate) * up).astype(BF16)

    def finish(s):
        rows = pl.ds(s * ts, ts)
        y = _dot(act_ref[s], wo_ref[...])
        o_ref[rows, :] = x_ref[rows, :] + _rms(y, half_gpost)

    h = h0_ref[...]
    for s in range(nsub):
        h_next = None
        for jb in range(nblk):
            hidden_block(s, h, jb)
            if jb == 0 and s + 1 < nsub:
                h_next = pre_norm(s + 1)
            if jb == 0 and s + 1 == nsub:
                h0_ref[...] = _rms(xnext_ref[...], gpre).astype(BF16)
        if s > 0:
            finish(s - 1)
        h = h_next
    finish(nsub - 1)


def _ffn_call(x2, gpre, gpost, w_in, w_out, *, layer, tm, nsub):
    n, d = x2.shape
    dff = w_out.shape[1]
    ts = tm // nsub
    const = lambda i: (0, 0)
    this_layer = lambda i: (layer, 0, 0)
    resident = dict(pipeline_mode=pl.Buffered(1))
    return pl.pallas_call(
        _ffn_body,
        grid=(n // tm,),
        in_specs=[
            pl.BlockSpec((tm, d), lambda i: (i, 0)),
            pl.BlockSpec((ts, d), lambda i: (jnp.minimum((i + 1) * nsub, n // ts - 1), 0)),
            pl.BlockSpec((1, d), const),
            pl.BlockSpec((1, d), const),
            pl.BlockSpec((None, d, 2 * dff), this_layer, **resident),
            pl.BlockSpec((None, dff, d), this_layer, **resident),
        ],
        out_specs=pl.BlockSpec((tm, d), lambda i: (i, 0)),
        out_shape=jax.ShapeDtypeStruct((n, d), F32),
        scratch_shapes=[pltpu.VMEM((nsub, ts, dff), BF16), pltpu.VMEM((ts, d), BF16)],
        compiler_params=pltpu.CompilerParams(
            dimension_semantics=("arbitrary",),
            vmem_limit_bytes=V7X_VMEM_LIMIT_BYTES),
        name="ffn_sublayer",
    )(x2, x2, gpre, gpost, w_in, w_out)


def _hgrn2_body(layer, x_ref, gpre_ref, gpost_ref, win_ref, lbraw_ref, ngain_ref, wout_ref,
                o_ref, proj_ref, y_ref, st_ref):
    tt, d = x_ref.shape[1], x_ref.shape[2]
    heads = d // HEAD_DIM

    @pl.when(pl.program_id(1) == 0)
    def _():
        st_ref[...] = jnp.zeros_like(st_ref)

    h = _rms(x_ref[0], gpre_ref[...]).astype(BF16)
    proj_ref[...] = _dot(h, win_ref[...])

    raw = lbraw_ref[...]
    e = jnp.exp(raw - jnp.max(raw, axis=0, keepdims=True))
    p = e / jnp.sum(e, axis=0, keepdims=True)
    first = p[0:1, :]
    cum = first
    for l in range(1, layer + 1):
        cum = cum + p[l:l + 1, :]
    lb = cum - first

    causal = _tril_mask(CHUNK)
    chunk_row = lax.broadcasted_iota(jnp.int32, (CHUNK, d), 0)
    scan_masks = [chunk_row >= (1 << k) for k in range(CHUNK.bit_length() - 1)]
    ngain = ngain_ref[...]

    def chunk_body(c, carry):
        rows = pl.ds(pl.multiple_of(c * CHUNK, CHUNK), CHUNK)
        q = jax.nn.silu(proj_ref[rows, pl.ds(0, d)])
        forget = lb + (1.0 - lb) * jax.nn.sigmoid(proj_ref[rows, pl.ds(d, d)])
        k = 1.0 - forget
        b = _cumsum_rows(jnp.log(forget), scan_masks)
        b_mid = b[CHUNK // 2:CHUNK // 2 + 1, :]
        b_last = b[CHUNK - 1:CHUNK, :]
        q_intra = q * jnp.exp(b - b_mid)
        k_intra = k * jnp.exp(b_mid - b)
        q_inter = q * jnp.exp(b)
        k_state = k * jnp.exp(b_last - b)
        decay = jnp.exp(b_last)
        cols = [slice(hh * HEAD_DIM, (hh + 1) * HEAD_DIM) for hh in range(heads)]
        vs = [proj_ref[rows, pl.ds(2 * d + hh * HEAD_DIM, HEAD_DIM)].astype(BF16) for hh in range(heads)]
        sts = [st_ref[hh] for hh in range(heads)]
        q_intra, k_intra = q_intra.astype(BF16), k_intra.astype(BF16)
        q_inter, k_state = q_inter.astype(BF16), k_state.astype(BF16)
        scores = [_dot_nt(q_intra[:, cs], k_intra[:, cs]) for cs in cols]
        inter = [_dot_nt(q_inter[:, cs], st.astype(BF16)) for cs, st in zip(cols, sts)]
        update = [_dot_tn(v, k_state[:, cs]) for cs, v in zip(cols, vs)]
        for hh in range(heads):
            cs = cols[hh]
            o = _dot(jnp.where(causal, scores[hh], 0.0).astype(BF16), vs[hh]) + inter[hh]
            st_ref[hh] = sts[hh] * decay[:, cs] + update[hh]
            gate = proj_ref[rows, pl.ds(3 * d + hh * HEAD_DIM, HEAD_DIM)]
            y = o * lax.rsqrt(jnp.mean(o * o, axis=-1, keepdims=True) + NORM_EPS)
            y_ref[rows, pl.ds(hh * HEAD_DIM, HEAD_DIM)] = (y * ngain * jax.nn.silu(gate)).astype(BF16)
        return carry

    lax.fori_loop(0, tt // CHUNK, chunk_body, 0, unroll=True)
    m = _dot(y_ref[...], wout_ref[...])
    o_ref[0] = x_ref[0] + _rms(m, gpost_ref[...])


def _hgrn2_call(x, gpre, gpost, w_in, lb_raw, ngain, w_out, *, layer, tt):
    bsz, t, d = x.shape
    heads = d // HEAD_DIM
    nl = lb_raw.shape[0]
    const = lambda b, i: (0, 0)
    this_layer = lambda b, i: (layer, 0, 0)
    return pl.pallas_call(
        functools.partial(_hgrn2_body, layer),
        grid=(bsz, t // tt),
        in_specs=[
            pl.BlockSpec((1, tt, d), lambda b, i: (b, i, 0)),
            pl.BlockSpec((1, d), const),
            pl.BlockSpec((1, d), const),
            pl.BlockSpec((None, d, 4 * d), this_layer),
            pl.BlockSpec((nl, d), const),
            pl.BlockSpec((1, HEAD_DIM), const),
            pl.BlockSpec((None, d, d), this_layer),
        ],
        out_specs=pl.BlockSpec((1, tt, d), lambda b, i: (b, i, 0)),
        out_shape=jax.ShapeDtypeStruct((bsz, t, d), F32),
        scratch_shapes=[
            pltpu.VMEM((tt, 4 * d), F32),
            pltpu.VMEM((tt, d), BF16),
            pltpu.VMEM((heads, HEAD_DIM, HEAD_DIM), F32),
        ],
        compiler_params=pltpu.CompilerParams(
            dimension_semantics=("parallel", "arbitrary"),
            vmem_limit_bytes=V7X_VMEM_LIMIT_BYTES),
        name="hgrn2_sublayer",
    )(x, gpre, gpost, w_in, lb_raw, ngain, w_out)


def _packed_matmul(xs, ys, left_half):
    out = []
    for x, y in zip(xs, ys):
        yb = y.astype(BF16)
        zero = jnp.zeros_like(yb)
        block_diag = jnp.concatenate([jnp.where(left_half, yb, zero), jnp.where(left_half, zero, yb)], axis=0)
        out.append(_dot(x.astype(BF16), block_diag))
    return out


def _unit_lower_inverse_minus_eye(a_list, diag_blocks, left_half):
    mm = functools.partial(_packed_matmul, left_half=left_half)

    dg = [jnp.where(diag_blocks, a, 0.0) for a in a_list]
    off = [a - d for a, d in zip(a_list, dg)]
    n = len(a_list)
    q = [-d for d in dg]
    p = mm(q, q)
    steps = INV_BLOCK.bit_length() - 2
    for step in range(steps):
        last = step == steps - 1
        prods = mm(p if last else p + p, q if last else q + p)
        q = [qi + pi + pqi for qi, pi, pqi in zip(q, p, prods[:n])]
        p = prods[n:]
    x = [-(o + qo) for o, qo in zip(off, mm(q, off))]
    x2 = mm(x, x)
    y = [xi + x2i + x3i for xi, x2i, x3i in zip(x, x2, mm(x, x2))]
    return [qi + yi + yqi for qi, yi, yqi in zip(q, y, mm(y, q))]


def _gdn_body(x_ref, gpre_ref, gpost_ref, win_ref, convw_ref, alog_ref, dtb_ref,
              ngain_ref, wout_ref, o_ref, ext_ref, hist_ref, qkv_ref, z_ref, y_ref, s_ref,
              gcol_ref, grow_ref, beta_ref, u_ref, lhs_a_ref, lhs_b_ref):
    tt, d = x_ref.shape[1], x_ref.shape[2]
    vheads = beta_ref.shape[1]
    nconv = qkv_ref.shape[1]
    key = (nconv - vheads * HEAD_DIM) // 2
    qk_heads = key // HEAD_DIM
    assert vheads == 2 * qk_heads
    assert key % PROJ_COL_GROUP == 0 and z_ref.shape[1] % PROJ_COL_GROUP == 0
    nchunk = tt // CHUNK
    t_idx = pl.program_id(1)

    @pl.when(t_idx == 0)
    def _():
        s_ref[...] = jnp.zeros_like(s_ref)
        hist_ref[...] = jnp.zeros_like(hist_ref)

    h = _rms(x_ref[0], gpre_ref[...]).astype(BF16)
    n_main = nconv + z_ref.shape[1]
    pba = _dot(h, win_ref[:, pl.ds(n_main, 2 * vheads)])
    beta_ref[...] = jax.nn.sigmoid(pba[:, :vheads])
    g = -jnp.exp(alog_ref[...]) * jax.nn.softplus(pba[:, vheads:] + dtb_ref[...])

    def project(c0):
        cols = pl.ds(c0, PROJ_COL_GROUP)
        pm = _dot(h, win_ref[:, cols])
        if c0 >= nconv:
            z_ref[:, pl.ds(c0 - nconv, PROJ_COL_GROUP)] = pm
            return
        ext_ref[pl.ds(0, CONV_HIST), :] = hist_ref[:, cols]
        ext_ref[pl.ds(CONV_HIST, tt), :] = pm
        hist_ref[:, cols] = pm[tt - CONV_HIST:, :]
        xe = ext_ref[...]
        acc = convw_ref[0:1, cols] * xe
        for j in range(1, CONV_K):
            acc = convw_ref[j:j + 1, cols] * xe + pltpu.roll(acc, shift=1, axis=0)
        act = jax.nn.silu(acc[CONV_HIST:, :])
        if c0 >= 2 * key:
            qkv_ref[:, cols] = act
            return
        for hh in range(PROJ_COL_GROUP // HEAD_DIM):
            a = act[:, hh * HEAD_DIM:(hh + 1) * HEAD_DIM]
            a = a * lax.rsqrt(jnp.sum(a * a, axis=-1, keepdims=True) + NORM_EPS)
            if c0 < key:
                a = a * (HEAD_DIM ** -0.5)
            qkv_ref[:, pl.ds(c0 + hh * HEAD_DIM, HEAD_DIM)] = a

    for c0 in range(0, n_main, PROJ_COL_GROUP):
        project(c0)

    r16 = lax.broadcasted_iota(jnp.int32, (vheads, vheads), 0)
    c16 = lax.broadcasted_iota(jnp.int32, (vheads, vheads), 1)
    eye_h = (r16 == c16).astype(F32)
    row_in_chunk = lax.broadcasted_iota(jnp.int32, (tt, vheads), 0) % CHUNK
    g_cum = _cumsum_rows(g, [row_in_chunk >= (1 << k) for k in range(CHUNK.bit_length() - 1)])
    gcol_ref[...] = g_cum
    for c in range(nchunk):
        grow_ref[c] = _dot_nt(eye_h, g_cum[c * CHUNK:(c + 1) * CHUNK, :], precision=HI)

    row = lax.broadcasted_iota(jnp.int32, (CHUNK, 2 * CHUNK), 0)
    lane = lax.broadcasted_iota(jnp.int32, (CHUNK, 2 * CHUNK), 1)
    col = lane % CHUNK
    left_half = lane < CHUNK
    incl = col <= row
    strict = col < row
    diag_blocks = (row // INV_BLOCK) == (col // INV_BLOCK)
    left_row = lax.broadcasted_iota(jnp.int32, (1, 2 * CHUNK), 1) < CHUNK
    ngain = ngain_ref[...]

    def prepare_chunks(ci):
        a_list, ctx = [], []
        for cc in range(PREP_CHUNKS):
            c = ci * PREP_CHUNKS + cc
            rows = _chunk_rows(c)
            g_cols = gcol_ref[rows, :]
            g_rows = grow_ref[c]
            betas = beta_ref[rows, :]
            for hq in range(qk_heads):
                q = qkv_ref[rows, pl.ds(hq * HEAD_DIM, HEAD_DIM)]
                k = qkv_ref[rows, pl.ds(key + hq * HEAD_DIM, HEAD_DIM)]
                kb = k.astype(BF16)
                prods = _dot_nt(jnp.concatenate([kb, q.astype(BF16)], axis=0),
                                jnp.concatenate([kb, kb], axis=0))
                heads = (2 * hq, 2 * hq + 1)
                g_full = [jnp.broadcast_to(g_cols[:, hv:hv + 1], (CHUNK, HEAD_DIM)) for hv in heads]
                b_full = [jnp.broadcast_to(betas[:, hv:hv + 1], (CHUNK, HEAD_DIM)) for hv in heads]
                g_col = jnp.where(left_half, g_full[0], g_full[1])
                b_col = jnp.where(left_half, b_full[0], b_full[1])
                g_row = jnp.concatenate([g_rows[hv:hv + 1, :] for hv in heads], axis=1)
                gamma = jnp.exp(jnp.where(incl, g_col - g_row, -jnp.inf))
                a_list.append(jnp.where(strict, prods[:CHUNK] * gamma * b_col, 0.0))
                ctx.append((cc, rows, hq, q, k, g_rows, g_row, g_full, b_full, prods[CHUNK:] * gamma))
        t_list = _unit_lower_inverse_minus_eye(a_list, diag_blocks, left_half)
        for t_m, (cc, rows, hq, q, k, g_rows, g_row, g_full, b_full, qk_gamma) in zip(t_list, ctx):
            heads = (2 * hq, 2 * hq + 1)
            k_t = k.T
            exp_g = [jnp.exp(gf) for gf in g_full]
            rhs = [jnp.concatenate([qkv_ref[rows, pl.ds(2 * key + hv * HEAD_DIM, HEAD_DIM)] * bf,
                                    k * (bf * eg)], axis=1)
                   for hv, bf, eg in zip(heads, b_full, exp_g)]
            rhs_both = jnp.concatenate(rhs, axis=0).astype(BF16)
            t_b = t_m.astype(BF16)
            zero = jnp.zeros_like(t_b)
            t_rhs = _dot(jnp.concatenate([jnp.where(left_half, t_b, zero), jnp.where(left_half, zero, t_b)], axis=0),
                         rhs_both)
            uw = [rhs[0] + t_rhs[:CHUNK], rhs[1] + t_rhs[CHUNK:]]
            u_ref[cc, hq] = jnp.concatenate([uw[0][:, :HEAD_DIM], uw[1][:, :HEAD_DIM]], axis=1)
            lhs_a_ref[cc, hq, pl.ds(0, CHUNK), :] = jnp.concatenate(
                [uw[0][:, HEAD_DIM:], uw[1][:, HEAD_DIM:]], axis=1).astype(BF16)
            lhs_a_ref[cc, hq, pl.ds(CHUNK, CHUNK), :] = jnp.concatenate(
                [q * exp_g[0], q * exp_g[1]], axis=1).astype(BF16)
            g_last = jnp.where(left_row, g_rows[heads[0]:heads[0] + 1, CHUNK - 1:CHUNK],
                               g_rows[heads[1]:heads[1] + 1, CHUNK - 1:CHUNK])
            lhs_b_ref[cc, hq, pl.ds(0, CHUNK), :] = qk_gamma.astype(BF16)
            lhs_b_ref[cc, hq, pl.ds(CHUNK, HEAD_DIM), :] = (
                jnp.concatenate([k_t, k_t], axis=1) * jnp.exp(g_last - g_row)).astype(BF16)

    left_head = lax.broadcasted_iota(jnp.int32, (1, 2 * HEAD_DIM), 1) < HEAD_DIM

    def block_diag(pair):
        zero = jnp.zeros_like(pair)
        return jnp.concatenate([jnp.where(left_head, pair, zero), jnp.where(left_head, zero, pair)], axis=0)

    def recur_chunk(c, cc):
        rows = _chunk_rows(c)
        decays = jnp.exp(grow_ref[c][:, CHUNK - 1:CHUNK])
        states = [s_ref[hq] for hq in range(qk_heads)]
        ws_qs = [_dot(lhs_a_ref[cc, hq], block_diag(states[hq].astype(BF16))) for hq in range(qk_heads)]
        for hq in range(qk_heads):
            v_new = u_ref[cc, hq] - ws_qs[hq][:CHUNK]
            upd = _dot(lhs_b_ref[cc, hq], block_diag(v_new.astype(BF16)))
            o_pair = ws_qs[hq][CHUNK:] + upd[:CHUNK]
            decay = jnp.where(left_head, decays[2 * hq:2 * hq + 1, :], decays[2 * hq + 1:2 * hq + 2, :])
            s_ref[hq] = decay * states[hq] + upd[CHUNK:]
            for half in range(2):
                lanes = pl.ds((2 * hq + half) * HEAD_DIM, HEAD_DIM)
                o = o_pair[:, half * HEAD_DIM:(half + 1) * HEAD_DIM]
                y = o * lax.rsqrt(jnp.mean(o * o, axis=-1, keepdims=True) + NORM_EPS)
                y_ref[rows, lanes] = (y * ngain * jax.nn.silu(z_ref[rows, lanes])).astype(BF16)

    def chunk_group(gi, carry):
        prepare_chunks(gi)
        for cc in range(PREP_CHUNKS):
            recur_chunk(gi * PREP_CHUNKS + cc, cc)
        return carry

    if nchunk == PREP_CHUNKS:
        chunk_group(0, 0)
    else:
        lax.fori_loop(0, nchunk // PREP_CHUNKS, chunk_group, 0)
    m = _dot(y_ref[...], wout_ref[...])
    o_ref[0] = x_ref[0] + _rms(m, gpost_ref[...])


def _gdn_call(x, gpre, gpost, w_in, conv_w, a_log, dt_bias, ngain, w_out, *, layer, tt):
    bsz, t, d = x.shape
    vheads = a_log.shape[1]
    nconv = conv_w.shape[1]
    val = vheads * HEAD_DIM
    nchunk = tt // CHUNK
    const = lambda b, i: (0, 0)
    this_layer = lambda b, i: (layer, 0, 0)
    resident = dict(pipeline_mode=pl.Buffered(1))
    return pl.pallas_call(
        _gdn_body,
        grid=(bsz, t // tt),
        in_specs=[
            pl.BlockSpec((1, tt, d), lambda b, i: (b, i, 0)),
            pl.BlockSpec((1, d), const),
            pl.BlockSpec((1, d), const),
            pl.BlockSpec((None, d, w_in.shape[2]), this_layer, **resident),
            pl.BlockSpec((CONV_K, nconv), const),
            pl.BlockSpec((1, vheads), const),
            pl.BlockSpec((1, vheads), const),
            pl.BlockSpec((1, HEAD_DIM), const),
            pl.BlockSpec((None, val, d), this_layer, **resident),
        ],
        out_specs=pl.BlockSpec((1, tt, d), lambda b, i: (b, i, 0)),
        out_shape=jax.ShapeDtypeStruct((bsz, t, d), F32),
        scratch_shapes=[
            pltpu.VMEM((tt + CONV_HIST, PROJ_COL_GROUP), F32),
            pltpu.VMEM((CONV_HIST, nconv), F32),
            pltpu.VMEM((tt, nconv), F32),
            pltpu.VMEM((tt, val), F32),
            pltpu.VMEM((tt, val), BF16),
            pltpu.VMEM((vheads // 2, HEAD_DIM, 2 * HEAD_DIM), F32),
            pltpu.VMEM((tt, vheads), F32),
            pltpu.VMEM((nchunk, vheads, CHUNK), F32),
            pltpu.VMEM((tt, vheads), F32),
            pltpu.VMEM((PREP_CHUNKS, vheads // 2, CHUNK, 2 * HEAD_DIM), F32),
            pltpu.VMEM((PREP_CHUNKS, vheads // 2, 2 * CHUNK, 2 * HEAD_DIM), BF16),
            pltpu.VMEM((PREP_CHUNKS, vheads // 2, CHUNK + HEAD_DIM, 2 * CHUNK), BF16),
        ],
        compiler_params=pltpu.CompilerParams(
            dimension_semantics=("parallel", "arbitrary"),
            vmem_limit_bytes=V7X_VMEM_LIMIT_BYTES),
        name="gdn_sublayer",
    )(x, gpre, gpost, w_in, conv_w, a_log, dt_bias, ngain, w_out)


def _tile(n, pref):
    t = min(n, pref)
    assert n % t == 0, (n, t)
    return t


def kernel(x, norm_gains, ffn1_w_in, ffn1_w_out, ffn2_w_in, ffn2_w_out, hg_w_in, hg_lower_bounds,
           hg_norm_gain, hg_w_out, gd_w_in, gd_conv_w, gd_a_log, gd_dt_bias, gd_norm_gain, gd_w_out):
    bsz, t, d = x.shape
    depth = norm_gains.shape[0]
    n_mixers = 2
    dff = ffn1_w_out.shape[1]

    tm = _tile(bsz * t, 1024)
    assert dff % FFN_COL_BLOCK == 0
    tt_hg = _tile(t, 512)
    tt_gd = _tile(t, 256)

    def row(v):
        return v.reshape(1, -1)

    ffn1_w_in, ffn1_w_out, ffn2_w_in, ffn2_w_out, hg_w_in, hg_w_out, gd_w_in, gd_w_out = (
        w.astype(BF16) for w in (ffn1_w_in, ffn1_w_out, ffn2_w_in, ffn2_w_out, hg_w_in, hg_w_out, gd_w_in, gd_w_out))

    def ffn(xc, gpre, gpost, w_in, w_out, layer):
        y = _ffn_call(xc.reshape(bsz * t, d), row(gpre), row(gpost), w_in, w_out, layer=layer, tm=tm, nsub=4)
        return y.reshape(bsz, t, d)

    for i in range(depth):
        ng = norm_gains[i]
        j = i // n_mixers
        x = ffn(x, ng[0], ng[1], ffn1_w_in, ffn1_w_out, i)
        if i % n_mixers == 0:
            x = _hgrn2_call(x, row(ng[2]), row(ng[3]), hg_w_in, hg_lower_bounds,
                            row(hg_norm_gain[j]), hg_w_out, layer=j, tt=tt_hg)
        else:
            x = _gdn_call(x, row(ng[2]), row(ng[3]), gd_w_in, gd_conv_w[j], row(gd_a_log[j]),
                          row(gd_dt_bias[j]), row(gd_norm_gain[j]), gd_w_out, layer=j, tt=tt_gd)
        x = ffn(x, ng[4], ng[5], ffn2_w_in, ffn2_w_out, i)
    return x
```
